```python
import jax, jax.numpy as jnp
from jax import lax
import numpy as np

D_MODEL = 1024
BATCH = 8
SEQ = 2048
DEPTH = 2

CHUNK = 64
Q_BLOCK = 128
N_A = DEPTH // 2
N_B = DEPTH - N_A
CONV_W = 3
N_HEADS = 8
QK_NOPE = 128
QK_ROPE = 64
V_HEAD = 128
Q_LORA = 384
KV_LORA = 256
D_FF = 2816
ROPE_THETA = 10000.0
EPS = 1e-6
NEG_INF = -1e30
MAX_POS_OFFSET = 8192

kernel_name = 'yoco_shortconv_mla_convffn'


def rms_norm(x, g):
    xf = x.astype(jnp.float32)
    y = xf * lax.rsqrt(jnp.mean(xf * xf, axis=-1, keepdims=True) + EPS)
    return (y * g.astype(jnp.float32)).astype(x.dtype)


def causal_dwconv(x, w):
    s = x.shape[1]
    xp = jnp.pad(x, ((0, 0), (CONV_W - 1, 0), (0, 0)))
    y = xp[:, 0:s, :] * w[0]
    for j in range(1, CONV_W):
        y = y + xp[:, j:j + s, :] * w[j]
    return y


def rope(x, positions):
    half = QK_ROPE // 2
    inv_freq = 1.0 / (ROPE_THETA ** (jnp.arange(half, dtype=jnp.float32) / half))
    ang = positions.astype(jnp.float32)[..., None] * inv_freq
    cos, sin = jnp.cos(ang), jnp.sin(ang)
    if x.ndim == 4:
        cos, sin = cos[:, :, None, :], sin[:, :, None, :]
    x1 = x[..., :half].astype(jnp.float32)
    x2 = x[..., half:].astype(jnp.float32)
    out = jnp.concatenate([x1 * cos - x2 * sin, x2 * cos + x1 * sin], axis=-1)
    return out.astype(x.dtype)


def short_conv_mixer(h, w_in, conv_w, w_out):
    b_gate, c_gate, u = jnp.split(h @ w_in, 3, axis=-1)
    return (b_gate * causal_dwconv(c_gate * u, conv_w)) @ w_out


def conv_ffn(h, w_up, conv_w, conv_b, w_down):
    g, v = jnp.split(h @ w_up, 2, axis=-1)
    g = causal_dwconv(g, conv_w) + conv_b
    return (jax.nn.silu(g) * v) @ w_down


def shared_kv(h, kv_in_norm, w_dkv, kv_latent_norm, w_kr, w_uk, w_uv, positions):
    b, s, _ = h.shape
    hn = rms_norm(h, kv_in_norm)
    c_kv = rms_norm(hn @ w_dkv, kv_latent_norm)
    k_rope = rope(hn @ w_kr, positions)
    k_nope = (c_kv @ w_uk).reshape(b, s, N_HEADS, QK_NOPE)
    v = (c_kv @ w_uv).reshape(b, s, N_HEADS, V_HEAD)
    return k_nope, k_rope, v


def mla_attention(h, w_dq, q_latent_norm, w_uq, w_o, k_nope, k_rope, v, positions):
    b, s, _ = h.shape
    c_q = rms_norm(h @ w_dq, q_latent_norm)
    q = (c_q @ w_uq).reshape(b, s, N_HEADS, QK_NOPE + QK_ROPE)
    q_nope = q[..., :QK_NOPE]
    q_rope = rope(q[..., QK_NOPE:], positions)
    scale = (QK_NOPE + QK_ROPE) ** -0.5
    nb = s // Q_BLOCK
    qn_blocks = q_nope.reshape(b, nb, Q_BLOCK, N_HEADS, QK_NOPE).transpose(1, 0, 2, 3, 4)
    qr_blocks = q_rope.reshape(b, nb, Q_BLOCK, N_HEADS, QK_ROPE).transpose(1, 0, 2, 3, 4)
    key_chunk = jnp.arange(s) // CHUNK

    def attend_block(args):
        qn, qr, blk = args
        sc = (jnp.einsum('bqhd,bkhd->bhqk', qn, k_nope)
              + jnp.einsum('bqhr,bkr->bhqk', qr, k_rope)).astype(jnp.float32) * scale
        q_chunk = (blk * Q_BLOCK + jnp.arange(Q_BLOCK)) // CHUNK
        mask = key_chunk[None, :] <= q_chunk[:, None]
        sc = jnp.where(mask[None, None], sc, NEG_INF)
        p = jax.nn.softmax(sc, axis=-1).astype(v.dtype)
        return jnp.einsum('bhqk,bkhd->bqhd', p, v)

    o = lax.map(attend_block, (qn_blocks, qr_blocks, jnp.arange(nb)))
    o = o.transpose(1, 0, 2, 3, 4).reshape(b, s, N_HEADS * V_HEAD)
    return o @ w_o


def setup_inputs(seed: int = 0) -> dict:
    key = jax.random.key(seed)
    ks = jax.random.split(key, 32)
    f32 = jnp.float32
    resid = (2 * DEPTH) ** -0.5

    def w(k, shape, fan_in, extra=1.0):
        return jax.random.normal(k, shape, f32) * (fan_in ** -0.5) * extra

    def gain(k, shape):
        return 1.0 + 0.02 * jax.random.normal(k, shape, f32)

    x = jax.random.normal(ks[0], (BATCH, SEQ, D_MODEL), f32)
    offsets = jax.random.randint(ks[1], (BATCH, 1), 0, MAX_POS_OFFSET, dtype=jnp.int32)
    positions = offsets + jnp.arange(SEQ, dtype=jnp.int32)[None, :]
    return {
        'x': x,
        'positions': positions,
        'attn_norm': gain(ks[2], (DEPTH, D_MODEL)),
        'ffn_norm': gain(ks[3], (DEPTH, D_MODEL)),
        'final_norm': gain(ks[4], (D_MODEL,)),
        'sc_w_in': w(ks[5], (N_A, D_MODEL, 3 * D_MODEL), D_MODEL),
        'sc_conv_w': w(ks[6], (N_A, CONV_W, D_MODEL), CONV_W),
        'sc_w_out': w(ks[7], (N_A, D_MODEL, D_MODEL), D_MODEL, resid),
        'kv_in_norm': gain(ks[8], (D_MODEL,)),
        'w_dkv': w(ks[9], (D_MODEL, KV_LORA), D_MODEL),
        'kv_latent_norm': gain(ks[10], (KV_LORA,)),
        'w_kr': w(ks[11], (D_MODEL, QK_ROPE), D_MODEL),
        'w_uk': w(ks[12], (KV_LORA, N_HEADS * QK_NOPE), KV_LORA),
        'w_uv': w(ks[13], (KV_LORA, N_HEADS * V_HEAD), KV_LORA),
        'w_dq': w(ks[14], (N_B, D_MODEL, Q_LORA), D_MODEL),
        'q_latent_norm': gain(ks[15], (N_B, Q_LORA)),
        'w_uq': w(ks[16], (N_B, Q_LORA, N_HEADS * (QK_NOPE + QK_ROPE)), Q_LORA),
        'w_o': w(ks[17], (N_B, N_HEADS * V_HEAD, D_MODEL), N_HEADS * V_HEAD, resid),
        'ffn_w_up': w(ks[18], (DEPTH, D_MODEL, 2 * D_FF), D_MODEL),
        'ffn_conv_w': w(ks[19], (DEPTH, CONV_W, D_FF), CONV_W),
        'ffn_conv_b': 0.02 * jax.random.normal(ks[20], (DEPTH, D_FF), f32),
        'ffn_w_down': w(ks[21], (DEPTH, D_FF, D_MODEL), D_FF, resid),
    }


def reference(x, positions, attn_norm, ffn_norm, final_norm, sc_w_in, sc_conv_w, sc_w_out,
              kv_in_norm, w_dkv, kv_latent_norm, w_kr, w_uk, w_uv,
              w_dq, q_latent_norm, w_uq, w_o,
              ffn_w_up, ffn_conv_w, ffn_conv_b, ffn_w_down):
    h = x
    kv = None
    for layer in range(DEPTH):
        hn = rms_norm(h, attn_norm[layer])
        if layer < N_A:
            h = h + short_conv_mixer(hn, sc_w_in[layer], sc_conv_w[layer], sc_w_out[layer])
        else:
            i = layer - N_A
            k_nope, k_rope, v = kv
            h = h + mla_attention(hn, w_dq[i], q_latent_norm[i], w_uq[i], w_o[i],
                                  k_nope, k_rope, v, positions)
        h = h + conv_ffn(rms_norm(h, ffn_norm[layer]), ffn_w_up[layer], ffn_conv_w[layer],
                         ffn_conv_b[layer], ffn_w_down[layer])
        if layer == N_A - 1:
            kv = shared_kv(h, kv_in_norm, w_dkv, kv_latent_norm, w_kr, w_uk, w_uv, positions)
    return rms_norm(h, final_norm)
```

```python
import functools

import jax
import jax.numpy as jnp
from jax import lax
from jax.experimental import pallas as pl
from jax.experimental.pallas import tpu as pltpu

CHUNK = 64
CONV_W = 3
N_HEADS = 8
QK_NOPE = 128
QK_ROPE = 64
V_HEAD = 128
ROPE_THETA = 10000.0
EPS = 1e-6
NEG_INF = -1e30

V7X_SUBLANES = 8
V7X_LANES = 128
V7X_MXU_DIM = 256
V7X_VMEM_BYTES = 64 * 1024 * 1024

HEAD_W = QK_NOPE + 2 * QK_ROPE
ROW_TILE = 512
Q_TILE = 256
FF_CHUNK = V7X_MXU_DIM
VMEM_LIMIT = V7X_VMEM_BYTES - 8 * 1024 * 1024

_BF16 = jnp.bfloat16
_F32 = jnp.float32


def _dot(a, b):
    return jnp.dot(a, b, preferred_element_type=_F32)


def _rms(x, g):
    return x * lax.rsqrt(jnp.mean(x * x, axis=-1, keepdims=True) + EPS) * g


def _const_spec(shape):
    nd = len(shape)
    return pl.BlockSpec(shape, lambda *_: (0,) * nd, pipeline_mode=pl.Buffered(1))


def _row_spec(tile, width):
    return pl.BlockSpec((None, tile, width), lambda b, i: (b, i, 0))


def _params():
    return pltpu.CompilerParams(
        dimension_semantics=("arbitrary", "arbitrary"), vmem_limit_bytes=VMEM_LIMIT)


def _causal_conv3(buf_ref, cur, w, tile):
    halo = V7X_SUBLANES
    y = buf_ref[pl.ds(halo - 2, tile), :] * w[0:1]
    y = y + buf_ref[pl.ds(halo - 1, tile), :] * w[1:2]
    return y + cur * w[2:3]


def _mixer_kernel(x_ref, g_ref, win_ref, cw_ref, wout_ref, o_ref, cu_ref):
    tile, d = x_ref.shape
    halo = V7X_SUBLANES

    @pl.when(pl.program_id(1) == 0)
    def _():
        cu_ref[0:halo, :] = jnp.zeros((halo, d), _F32)

    x = x_ref[...]
    hn = _rms(x, g_ref[...]).astype(_BF16)
    b_gate = _dot(hn, win_ref[:, 0:d])
    cu = _dot(hn, win_ref[:, d:2 * d]) * _dot(hn, win_ref[:, 2 * d:3 * d])
    cu_ref[halo:halo + tile, :] = cu
    conv = _causal_conv3(cu_ref, cu, cw_ref[...], tile)
    cu_ref[0:halo, :] = cu_ref[tile:tile + halo, :]
    y = (b_gate * conv).astype(_BF16)
    o_ref[...] = x + _dot(y, wout_ref[...])


def _mixer(x, g, w_in, conv_w, w_out):
    bsz, s, d = x.shape
    tile = ROW_TILE
    return pl.pallas_call(
        _mixer_kernel,
        grid=(bsz, s // tile),
        in_specs=[_row_spec(tile, d), _const_spec((1, d)), _const_spec((d, 3 * d)),
                  _const_spec((CONV_W, d)), _const_spec((d, d))],
        out_specs=_row_spec(tile, d),
        out_shape=jax.ShapeDtypeStruct(x.shape, _F32),
        scratch_shapes=[pltpu.VMEM((tile + V7X_SUBLANES, d), _F32)],
        compiler_params=_params(),
        name="sc_mixer",
    )(x, g, w_in, conv_w, w_out)


def _ffn_kernel(h_ref, g_ref, wup_ref, cw_ref, cb_ref, wdn_ref, fg_ref, o_ref,
                gbuf_ref, act_ref, *, final_norm):
    tile, d = h_ref.shape
    dff = act_ref.shape[1]
    halo = V7X_SUBLANES
    ch = FF_CHUNK

    @pl.when(pl.program_id(1) == 0)
    def _():
        gbuf_ref[0:halo, :] = jnp.zeros((halo, dff), _F32)

    h = h_ref[...]
    hn = _rms(h, g_ref[...]).astype(_BF16)
    for c in range(dff // ch):
        cols = slice(c * ch, (c + 1) * ch)
        gv = _dot(hn, wup_ref[:, 2 * c * ch:2 * (c + 1) * ch])
        g = gv[:, 0:ch]
        gbuf_ref[halo:halo + tile, cols] = g
        pre = _causal_conv3(gbuf_ref.at[:, cols], g, cw_ref[:, cols], tile) + cb_ref[:, cols]
        gbuf_ref[0:halo, cols] = gbuf_ref[tile:tile + halo, cols]
        act = pre * (1.0 / (1.0 + jnp.exp(-pre))) * gv[:, ch:2 * ch]
        act_ref[:, cols] = act.astype(_BF16)
    out = h + _dot(act_ref[...], wdn_ref[...])
    if final_norm:
        out = _rms(out, fg_ref[...])
    o_ref[...] = out


def _ffn(h, g, w_up, conv_w, conv_b, w_down, final_g, *, final_norm):
    bsz, s, d = h.shape
    dff = w_down.shape[0]
    tile = ROW_TILE
    return pl.pallas_call(
        functools.partial(_ffn_kernel, final_norm=final_norm),
        grid=(bsz, s // tile),
        in_specs=[_row_spec(tile, d), _const_spec((1, d)), _const_spec((d, 2 * dff)),
                  _const_spec((CONV_W, dff)), _const_spec((1, dff)), _const_spec((dff, d)),
                  _const_spec((1, d))],
        out_specs=_row_spec(tile, d),
        out_shape=jax.ShapeDtypeStruct(h.shape, _F32),
        scratch_shapes=[pltpu.VMEM((tile + V7X_SUBLANES, dff), _F32),
                        pltpu.VMEM((tile, dff), _BF16)],
        compiler_params=_params(),
        name="conv_ffn_final" if final_norm else "conv_ffn",
    )(h, g, w_up, conv_w, conv_b, w_down, final_g)


def _rope_table(pos, inv_freq, scale):
    ang = pos.astype(_F32) * inv_freq
    lane = lax.broadcasted_iota(jnp.int32, ang.shape, 1)
    sin = jnp.sin(ang)
    table = jnp.where(lane < 2 * (QK_ROPE // 2), jnp.cos(ang),
                      jnp.where(lane < 3 * (QK_ROPE // 2), -sin, sin))
    return table * scale if scale != 1.0 else table


def _kv_kernel(h_ref, pos_ref, freq_ref, gin_ref, wdkv_ref, glat_ref, wkr_ref, wuk_ref,
               wuv_ref, kn_ref, kr_ref, v_ref):
    hn = _rms(h_ref[...], gin_ref[...]).astype(_BF16)
    c_kv = _rms(_dot(hn, wdkv_ref[...]), glat_ref[...]).astype(_BF16)
    p = _dot(hn, wkr_ref[...]) * _rope_table(pos_ref[...], freq_ref[...], 1.0)
    kr_ref[...] = (p + pltpu.roll(p, QK_ROPE, 1)).astype(_BF16)
    kn_ref[...] = _dot(c_kv, wuk_ref[...]).astype(_BF16)
    v_ref[...] = _dot(c_kv, wuv_ref[...]).astype(_BF16)


def _shared_kv(h, pos, freq, g_in, w_dkv, g_lat, w_kr2, w_uk, w_uv):
    bsz, s, d = h.shape
    lora = w_dkv.shape[1]
    hk = w_uk.shape[1]
    hv = w_uv.shape[1]
    tile = ROW_TILE
    return pl.pallas_call(
        _kv_kernel,
        grid=(bsz, s // tile),
        in_specs=[_row_spec(tile, d), _row_spec(tile, 1), _const_spec((1, 2 * QK_ROPE)),
                  _const_spec((1, d)), _const_spec((d, lora)), _const_spec((1, lora)),
                  _const_spec((d, 2 * QK_ROPE)), _const_spec((lora, hk)),
                  _const_spec((lora, hv))],
        out_specs=[_row_spec(tile, hk), _row_spec(tile, 2 * QK_ROPE), _row_spec(tile, hv)],
        out_shape=[jax.ShapeDtypeStruct((bsz, s, hk), _BF16),
                   jax.ShapeDtypeStruct((bsz, s, 2 * QK_ROPE), _BF16),
                   jax.ShapeDtypeStruct((bsz, s, hv), _BF16)],
        compiler_params=_params(),
        name="shared_kv",
    )(h, pos, freq, g_in, w_dkv, g_lat, w_kr2, w_uk, w_uv)


def _mla_kernel(h_ref, pos_ref, freq_ref, g_ref, wdq_ref, gq_ref, wuq_ref, wo_ref,
                kn_ref, kr_ref, v_ref, o_ref, q_ref, acc_ref, m_ref, l_ref):
    tq, d = h_ref.shape
    tk = tq
    i = pl.program_id(1)
    scale = float(QK_NOPE + QK_ROPE) ** -0.5

    h = h_ref[...]
    hn = _rms(h, g_ref[...]).astype(_BF16)
    c_q = _rms(_dot(hn, wdq_ref[...]), gq_ref[...]).astype(_BF16)
    table = _rope_table(pos_ref[...], freq_ref[...], scale)
    for hd in range(N_HEADS):
        q = _dot(c_q, wuq_ref[:, hd * HEAD_W:(hd + 1) * HEAD_W])
        q_ref[:, hd * HEAD_W:hd * HEAD_W + QK_NOPE] = (q[:, 0:QK_NOPE] * scale).astype(_BF16)
        q_ref[:, hd * HEAD_W + QK_NOPE:(hd + 1) * HEAD_W] = (q[:, QK_NOPE:] * table).astype(_BF16)

    m_ref[...] = jnp.full(m_ref.shape, NEG_INF, _F32)
    l_ref[...] = jnp.zeros(l_ref.shape, _F32)
    acc_ref[...] = jnp.zeros(acc_ref.shape, _F32)

    def attend(j, bias):
        rows = pl.ds(pl.multiple_of(j * tk, tk), tk)
        k_rope = kr_ref[rows, :]
        for hd in range(N_HEADS):
            k = jnp.concatenate([kn_ref[rows, hd * QK_NOPE:(hd + 1) * QK_NOPE], k_rope], axis=1)
            s = lax.dot_general(q_ref[:, hd * HEAD_W:(hd + 1) * HEAD_W], k,
                                (((1,), (1,)), ((), ())), preferred_element_type=_F32)
            if bias is not None:
                s = jnp.where(bias, s, NEG_INF)
            m_prev = m_ref[hd]
            m_new = jnp.maximum(m_prev, jnp.max(s, axis=-1, keepdims=True))
            alpha = jnp.exp(m_prev - m_new)
            p = jnp.exp(s - m_new)
            l_ref[hd] = alpha * l_ref[hd] + jnp.sum(p, axis=-1, keepdims=True)
            m_ref[hd] = m_new
            vcols = slice(hd * V_HEAD, (hd + 1) * V_HEAD)
            acc_ref[:, vcols] = alpha * acc_ref[:, vcols] + _dot(p.astype(_BF16), v_ref[rows, vcols])

    def body(j, carry):
        attend(j, None)
        return carry

    lax.fori_loop(0, i, body, 0)
    q_chunk = lax.broadcasted_iota(jnp.int32, (tq, tk), 0) // CHUNK
    k_chunk = lax.broadcasted_iota(jnp.int32, (tq, tk), 1) // CHUNK
    attend(i, k_chunk <= q_chunk)

    for hd in range(N_HEADS):
        vcols = slice(hd * V_HEAD, (hd + 1) * V_HEAD)
        acc_ref[:, vcols] = acc_ref[:, vcols] * (1.0 / l_ref[hd])
    o_ref[...] = h + _dot(acc_ref[...].astype(_BF16), wo_ref[...])


def _mla(h, pos, freq, g, w_dq, g_q, w_uq2, w_o, k_nope, k_rope, v):
    bsz, s, d = h.shape
    qlora = w_dq.shape[1]
    hk = k_nope.shape[2]
    hv = v.shape[2]
    tq = Q_TILE
    kv_spec = lambda width: pl.BlockSpec((None, s, width), lambda b, i: (b, 0, 0))
    return pl.pallas_call(
        _mla_kernel,
        grid=(bsz, s // tq),
        in_specs=[_row_spec(tq, d), _row_spec(tq, 1), _const_spec((1, 2 * QK_ROPE)),
                  _const_spec((1, d)), _const_spec((d, qlora)), _const_spec((1, qlora)),
                  _const_spec((qlora, N_HEADS * HEAD_W)), _const_spec((hv, d)),
                  kv_spec(hk), kv_spec(2 * QK_ROPE), kv_spec(hv)],
        out_specs=_row_spec(tq, d),
        out_shape=jax.ShapeDtypeStruct(h.shape, _F32),
        scratch_shapes=[pltpu.VMEM((tq, N_HEADS * HEAD_W), _BF16),
                        pltpu.VMEM((tq, hv), _F32),
                        pltpu.VMEM((N_HEADS, tq, 1), _F32),
                        pltpu.VMEM((N_HEADS, tq, 1), _F32)],
        compiler_params=_params(),
        name="mla_attention",
    )(h, pos, freq, g, w_dq, g_q, w_uq2, w_o, k_nope, k_rope, v)


def _swap_halves(w):
    half = w.shape[-1] // 2
    return jnp.concatenate([w[..., half:], w[..., :half]], axis=-1)


def _interleave_up(w_up, dff):
    d = w_up.shape[0]
    return w_up.reshape(d, 2, dff // FF_CHUNK, FF_CHUNK).transpose(0, 2, 1, 3).reshape(d, 2 * dff)


def kernel(x, positions, attn_norm, ffn_norm, final_norm, sc_w_in, sc_conv_w, sc_w_out, kv_in_norm, w_dkv, kv_latent_norm, w_kr, w_uk, w_uv, w_dq, q_latent_norm, w_uq, w_o, ffn_w_up, ffn_conv_w, ffn_conv_b, ffn_w_down):
    depth = attn_norm.shape[0]
    n_a = sc_w_in.shape[0]
    dff = ffn_w_down.shape[1]
    row = lambda v: v.reshape(1, -1)
    bf = lambda w: w.astype(_BF16)

    half = QK_ROPE // 2
    inv_freq = 1.0 / (ROPE_THETA ** (jnp.arange(half, dtype=_F32) / half))
    freq = jnp.tile(inv_freq, 4).reshape(1, 2 * QK_ROPE)
    pos = positions.reshape(positions.shape + (1,))

    w_kr2 = bf(jnp.concatenate([w_kr, _swap_halves(w_kr)], axis=1))
    final_g = row(final_norm)

    h = x
    kv = None
    for layer in range(depth):
        g_attn = row(attn_norm[layer])
        if layer < n_a:
            h = _mixer(h, g_attn, bf(sc_w_in[layer]), sc_conv_w[layer], bf(sc_w_out[layer]))
        else:
            a = layer - n_a
            wq = w_uq[a].reshape(-1, N_HEADS, QK_NOPE + QK_ROPE)
            rope_cols = wq[:, :, QK_NOPE:]
            w_uq2 = jnp.concatenate([wq, _swap_halves(rope_cols)], axis=2)
            w_uq2 = bf(w_uq2.reshape(-1, N_HEADS * HEAD_W))
            h = _mla(h, pos, freq, g_attn, bf(w_dq[a]), row(q_latent_norm[a]), w_uq2,
                     bf(w_o[a]), *kv)
        h = _ffn(h, row(ffn_norm[layer]), bf(_interleave_up(ffn_w_up[layer], dff)),
                 ffn_conv_w[layer], row(ffn_conv_b[layer]), bf(ffn_w_down[layer]), final_g,
                 final_norm=(layer == depth - 1))
        if layer == n_a - 1:
            kv = _shared_kv(h, pos, freq, row(kv_in_norm), bf(w_dkv), row(kv_latent_norm),
                            w_kr2, bf(w_uk), bf(w_uv))
    return h
```

```python
import functools
import math

import jax
import jax.numpy as jnp
from jax import lax
from jax.experimental import pallas as pl
from jax.experimental.pallas import tpu as pltpu

CHUNK = 64
CONV_W = 3
N_HEADS = 8
QK_NOPE = 128
QK_ROPE = 64
V_HEAD = 128
ROPE_THETA = 10000.0
EPS = 1e-6
NEG_INF = -1e30

V7X_SUBLANES = 8
V7X_LANES = 128
V7X_MXU_DIM = 256
V7X_VMEM_BYTES = 64 * 1024 * 1024

HEAD_W = QK_NOPE + 2 * QK_ROPE
ROW_TILE = 512
Q_TILE = 256
FF_CHUNK = V7X_MXU_DIM
VMEM_LIMIT = V7X_VMEM_BYTES - 8 * 1024 * 1024

_BF16 = jnp.bfloat16
_F32 = jnp.float32


def _dot(a, b):
    return jnp.dot(a, b, preferred_element_type=_F32)


def _rms(x, g):
    return x * lax.rsqrt(jnp.mean(x * x, axis=-1, keepdims=True) + EPS) * g


def _const_spec(shape):
    nd = len(shape)
    return pl.BlockSpec(shape, lambda *_: (0,) * nd, pipeline_mode=pl.Buffered(1))


def _row_spec(tile, width):
    return pl.BlockSpec((None, tile, width), lambda b, i: (b, i, 0))


def _params():
    return pltpu.CompilerParams(
        dimension_semantics=("arbitrary", "arbitrary"), vmem_limit_bytes=VMEM_LIMIT)


def _causal_conv3(buf_ref, cur, w, tile):
    halo = V7X_SUBLANES
    y = buf_ref[pl.ds(halo - 2, tile), :] * w[0:1]
    y = y + buf_ref[pl.ds(halo - 1, tile), :] * w[1:2]
    return y + cur * w[2:3]


def _mixer_kernel(x_ref, g_ref, win_ref, cw_ref, wout_ref, o_ref, cu_ref):
    tile, d = x_ref.shape
    halo = V7X_SUBLANES

    @pl.when(pl.program_id(1) == 0)
    def _():
        cu_ref[0:halo, :] = jnp.zeros((halo, d), _F32)

    x = x_ref[...]
    hn = _rms(x, g_ref[...]).astype(_BF16)
    b_gate = _dot(hn, win_ref[:, 0:d])
    cu = _dot(hn, win_ref[:, d:2 * d]) * _dot(hn, win_ref[:, 2 * d:3 * d])
    cu_ref[halo:halo + tile, :] = cu
    conv = _causal_conv3(cu_ref, cu, cw_ref[...], tile)
    cu_ref[0:halo, :] = cu_ref[tile:tile + halo, :]
    y = (b_gate * conv).astype(_BF16)
    o_ref[...] = x + _dot(y, wout_ref[...])


def _mixer(x, g, w_in, conv_w, w_out):
    bsz, s, d = x.shape
    tile = ROW_TILE
    return pl.pallas_call(
        _mixer_kernel,
        grid=(bsz, s // tile),
        in_specs=[_row_spec(tile, d), _const_spec((1, d)), _const_spec((d, 3 * d)),
                  _const_spec((CONV_W, d)), _const_spec((d, d))],
        out_specs=_row_spec(tile, d),
        out_shape=jax.ShapeDtypeStruct(x.shape, _F32),
        scratch_shapes=[pltpu.VMEM((tile + V7X_SUBLANES, d), _F32)],
        compiler_params=_params(),
        name="sc_mixer",
    )(x, g, w_in, conv_w, w_out)


def _ffn_kernel(h_ref, g_ref, wup_ref, cw_ref, cb_ref, wdn_ref, fg_ref, o_ref,
                gbuf_ref, act_ref, *, final_norm):
    tile, d = h_ref.shape
    dff = act_ref.shape[1]
    halo = V7X_SUBLANES
    ch = FF_CHUNK

    @pl.when(pl.program_id(1) == 0)
    def _():
        gbuf_ref[0:halo, :] = jnp.zeros((halo, dff), _F32)

    h = h_ref[...]
    hn = _rms(h, g_ref[...]).astype(_BF16)
    for c in range(dff // ch):
        cols = slice(c * ch, (c + 1) * ch)
        gv = _dot(hn, wup_ref[:, 2 * c * ch:2 * (c + 1) * ch])
        g = gv[:, 0:ch]
        gbuf_ref[halo:halo + tile, cols] = g
        pre = _causal_conv3(gbuf_ref.at[:, cols], g, cw_ref[:, cols], tile) + cb_ref[:, cols]
        gbuf_ref[0:halo, cols] = gbuf_ref[tile:tile + halo, cols]
        act = pre * (1.0 / (1.0 + jnp.exp(-pre))) * gv[:, ch:2 * ch]
        act_ref[:, cols] = act.astype(_BF16)
    out = h + _dot(act_ref[...], wdn_ref[...])
    if final_norm:
        out = _rms(out, fg_ref[...])
    o_ref[...] = out


def _ffn(h, g, w_up, conv_w, conv_b, w_down, final_g, *, final_norm):
    bsz, s, d = h.shape
    dff = w_down.shape[0]
    tile = ROW_TILE
    return pl.pallas_call(
        functools.partial(_ffn_kernel, final_norm=final_norm),
        grid=(bsz, s // tile),
        in_specs=[_row_spec(tile, d), _const_spec((1, d)), _const_spec((d, 2 * dff)),
                  _const_spec((CONV_W, dff)), _const_spec((1, dff)), _const_spec((dff, d)),
                  _const_spec((1, d))],
        out_specs=_row_spec(tile, d),
        out_shape=jax.ShapeDtypeStruct(h.shape, _F32),
        scratch_shapes=[pltpu.VMEM((tile + V7X_SUBLANES, dff), _F32),
                        pltpu.VMEM((tile, dff), _BF16)],
        compiler_params=_params(),
        name="conv_ffn_final" if final_norm else "conv_ffn",
    )(h, g, w_up, conv_w, conv_b, w_down, final_g)


def _rope_table(pos, inv_freq, scale, axis):
    ang = pos.astype(_F32) * inv_freq
    idx = lax.broadcasted_iota(jnp.int32, ang.shape, axis)
    sin = jnp.sin(ang)
    table = jnp.where(idx < 2 * (QK_ROPE // 2), jnp.cos(ang),
                      jnp.where(idx < 3 * (QK_ROPE // 2), -sin, sin))
    return table * scale if scale != 1.0 else table


def _kv_kernel(h_ref, pos_ref, freq_ref, gin_ref, wdkv_ref, glat_ref, wkr_ref, wuk_ref,
               wuvt_ref, kn_ref, kr_ref, vt_ref):
    hn = _rms(h_ref[...], gin_ref[...]).astype(_BF16)
    c_kv = _rms(_dot(hn, wdkv_ref[...]), glat_ref[...])
    p = _dot(hn, wkr_ref[...]) * _rope_table(pos_ref[...], freq_ref[...], 1.0, 1)
    kr_ref[...] = (p + pltpu.roll(p, QK_ROPE, 1)).astype(_BF16)
    kn_ref[...] = _dot(c_kv.astype(_BF16), wuk_ref[...]).astype(_BF16)
    vt_ref[...] = _dot(wuvt_ref[...], c_kv.T.astype(_BF16)).astype(_BF16)


def _shared_kv(h, pos, freq, g_in, w_dkv, g_lat, w_kr2, w_uk, w_uv_t):
    bsz, s, d = h.shape
    lora = w_dkv.shape[1]
    hk = w_uk.shape[1]
    hv = w_uv_t.shape[0]
    tile = ROW_TILE
    return pl.pallas_call(
        _kv_kernel,
        grid=(bsz, s // tile),
        in_specs=[_row_spec(tile, d), _row_spec(tile, 1), _const_spec((1, 2 * QK_ROPE)),
                  _const_spec((1, d)), _const_spec((d, lora)), _const_spec((1, lora)),
                  _const_spec((d, 2 * QK_ROPE)), _const_spec((lora, hk)),
                  _const_spec((hv, lora))],
        out_specs=[_row_spec(tile, hk), _row_spec(tile, 2 * QK_ROPE),
                   pl.BlockSpec((None, hv, tile), lambda b, i: (b, 0, i))],
        out_shape=[jax.ShapeDtypeStruct((bsz, s, hk), _BF16),
                   jax.ShapeDtypeStruct((bsz, s, 2 * QK_ROPE), _BF16),
                   jax.ShapeDtypeStruct((bsz, hv, s), _BF16)],
        compiler_params=_params(),
        name="shared_kv",
    )(h, pos, freq, g_in, w_dkv, g_lat, w_kr2, w_uk, w_uv_t)


def _mla_kernel(h_ref, posr_ref, freqc_ref, g_ref, wdq_ref, gq_ref, wuqt_ref, wo_ref,
                kn_ref, kr_ref, vt_ref, o_ref, qt_ref, ob_ref):
    tq, d = h_ref.shape
    i = pl.program_id(1)
    scale = float(QK_NOPE + QK_ROPE) ** -0.5 * math.log2(math.e)

    h = h_ref[...]
    hn = _rms(h, g_ref[...]).astype(_BF16)
    c_q = _rms(_dot(hn, wdq_ref[...]), gq_ref[...])
    c_qt = c_q.T.astype(_BF16)
    table = _rope_table(posr_ref[...], freqc_ref[...], scale, 0)
    for hd in range(N_HEADS):
        qt = _dot(wuqt_ref[hd * HEAD_W:(hd + 1) * HEAD_W, :], c_qt)
        qt_ref[hd * HEAD_W:hd * HEAD_W + QK_NOPE, :] = (qt[0:QK_NOPE] * scale).astype(_BF16)
        qt_ref[hd * HEAD_W + QK_NOPE:(hd + 1) * HEAD_W, :] = (qt[QK_NOPE:] * table).astype(_BF16)

    ones_rows = 2 * V7X_SUBLANES
    k_chunk = lax.broadcasted_iota(jnp.int32, (tq, tq), 0) // CHUNK
    q_chunk = lax.broadcasted_iota(jnp.int32, (tq, tq), 1) // CHUNK
    diag_mask = k_chunk <= q_chunk

    def attend(n_keys):
        ones = jnp.ones((ones_rows, n_keys), _BF16)
        k_rope = kr_ref[0:n_keys, :]

        def scores(hd):
            k = jnp.concatenate([kn_ref[0:n_keys, hd * QK_NOPE:(hd + 1) * QK_NOPE], k_rope], axis=1)
            return _dot(k, qt_ref[hd * HEAD_W:(hd + 1) * HEAD_W, :])

        s_next = scores(0)
        for hd in range(N_HEADS):
            s = s_next
            if hd + 1 < N_HEADS:
                s_next = scores(hd + 1)
            s_diag = jnp.where(diag_mask, s[n_keys - tq:], NEG_INF)
            s = jnp.concatenate([s[:n_keys - tq], s_diag], axis=0) if n_keys > tq else s_diag
            p = jnp.exp2(s - jnp.max(s, axis=0, keepdims=True)).astype(_BF16)
            v_aug = jnp.concatenate([vt_ref[hd * V_HEAD:(hd + 1) * V_HEAD, 0:n_keys], ones], axis=0)
            o_aug = _dot(v_aug, p)
            o_t = o_aug[0:V_HEAD] * (1.0 / o_aug[V_HEAD:V_HEAD + 1])
            ob_ref[:, hd * V_HEAD:(hd + 1) * V_HEAD] = o_t.T.astype(_BF16)

    for c in range(kn_ref.shape[0] // tq):
        pl.when(i == c)(functools.partial(attend, (c + 1) * tq))

    o_ref[...] = h + _dot(ob_ref[...], wo_ref[...])


def _mla(h, pos_row, freq_col, g, w_dq, g_q, w_uq2_t, w_o, k_nope, k_rope, v_t):
    bsz, s, d = h.shape
    qlora = w_dq.shape[1]
    hk = k_nope.shape[2]
    hv = v_t.shape[1]
    tq = Q_TILE
    return pl.pallas_call(
        _mla_kernel,
        grid=(bsz, s // tq),
        in_specs=[_row_spec(tq, d), pl.BlockSpec((None, 1, tq), lambda b, i: (b, 0, i)),
                  _const_spec((2 * QK_ROPE, 1)),
                  _const_spec((1, d)), _const_spec((d, qlora)), _const_spec((1, qlora)),
                  _const_spec((N_HEADS * HEAD_W, qlora)), _const_spec((hv, d)),
                  pl.BlockSpec((None, s, hk), lambda b, i: (b, 0, 0)),
                  pl.BlockSpec((None, s, 2 * QK_ROPE), lambda b, i: (b, 0, 0)),
                  pl.BlockSpec((None, hv, s), lambda b, i: (b, 0, 0))],
        out_specs=_row_spec(tq, d),
        out_shape=jax.ShapeDtypeStruct(h.shape, _F32),
        scratch_shapes=[pltpu.VMEM((N_HEADS * HEAD_W, tq), _BF16),
                        pltpu.VMEM((tq, hv), _BF16)],
        compiler_params=_params(),
        name="mla_attention",
    )(h, pos_row, freq_col, g, w_dq, g_q, w_uq2_t, w_o, k_nope, k_rope, v_t)


def _swap_halves(w):
    half = w.shape[-1] // 2
    return jnp.concatenate([w[..., half:], w[..., :half]], axis=-1)


def _interleave_up(w_up, dff):
    d = w_up.shape[0]
    return w_up.reshape(d, 2, dff // FF_CHUNK, FF_CHUNK).transpose(0, 2, 1, 3).reshape(d, 2 * dff)


def kernel(x, positions, attn_norm, ffn_norm, final_norm, sc_w_in, sc_conv_w, sc_w_out, kv_in_norm, w_dkv, kv_latent_norm, w_kr, w_uk, w_uv, w_dq, q_latent_norm, w_uq, w_o, ffn_w_up, ffn_conv_w, ffn_conv_b, ffn_w_down):
    depth = attn_norm.shape[0]
    n_a = sc_w_in.shape[0]
    dff = ffn_w_down.shape[1]
    row = lambda v: v.reshape(1, -1)
    bf = lambda w: w.astype(_BF16)

    half = QK_ROPE // 2
    inv_freq = 1.0 / (ROPE_THETA ** (jnp.arange(half, dtype=_F32) / half))
    freq = jnp.tile(inv_freq, 4).reshape(1, 2 * QK_ROPE)
    freq_col = freq.reshape(2 * QK_ROPE, 1)
    pos = positions.reshape(positions.shape + (1,))
    pos_row = positions.reshape(positions.shape[0], 1, positions.shape[1])

    w_kr2 = bf(jnp.concatenate([w_kr, _swap_halves(w_kr)], axis=1))
    final_g = row(final_norm)

    h = x
    kv = None
    for layer in range(depth):
        g_attn = row(attn_norm[layer])
        if layer < n_a:
            h = _mixer(h, g_attn, bf(sc_w_in[layer]), sc_conv_w[layer], bf(sc_w_out[layer]))
        else:
            a = layer - n_a
            wq = w_uq[a].reshape(-1, N_HEADS, QK_NOPE + QK_ROPE)
            rope_cols = wq[:, :, QK_NOPE:]
            w_uq2 = jnp.concatenate([wq, _swap_halves(rope_cols)], axis=2)
            w_uq2_t = bf(w_uq2.reshape(-1, N_HEADS * HEAD_W).T)
            h = _mla(h, pos_row, freq_col, g_attn, bf(w_dq[a]), row(q_latent_norm[a]), w_uq2_t,
                     bf(w_o[a]), *kv)
        h = _ffn(h, row(ffn_norm[layer]), bf(_interleave_up(ffn_w_up[layer], dff)),
                 ffn_conv_w[layer], row(ffn_conv_b[layer]), bf(ffn_w_down[layer]), final_g,
                 final_norm=(layer == depth - 1))
        if layer == n_a - 1:
            kv = _shared_kv(h, pos, freq, row(kv_in_norm), bf(w_dkv), row(kv_latent_norm),
                            w_kr2, bf(w_uk), bf(w_uv.T))
    return h
```

```python
import functools
import math

import jax
import jax.numpy as jnp
from jax import lax
from jax.experimental import pallas as pl
from jax.experimental.pallas import tpu as pltpu

CHUNK = 64
CONV_W = 3
N_HEADS = 8
QK_NOPE = 128
QK_ROPE = 64
V_HEAD = 128
ROPE_THETA = 10000.0
EPS = 1e-6
NEG_INF = -1e30

V7X_SUBLANES = 8
V7X_LANES = 128
V7X_MXU_DIM = 256
V7X_VMEM_BYTES = 64 * 1024 * 1024

HEAD_W = QK_NOPE + 2 * QK_ROPE
ROW_TILE = 512
Q_TILE = 256
FF_CHUNK = V7X_MXU_DIM
VMEM_LIMIT = V7X_VMEM_BYTES - 8 * 1024 * 1024

_BF16 = jnp.bfloat16
_F32 = jnp.float32


def _dot(a, b):
    return jnp.dot(a, b, preferred_element_type=_F32)


def _rms(x, g):
    return x * lax.rsqrt(jnp.mean(x * x, axis=-1, keepdims=True) + EPS) * g


def _const_spec(shape):
    nd = len(shape)
    return pl.BlockSpec(shape, lambda *_: (0,) * nd, pipeline_mode=pl.Buffered(1))


def _row_spec(tile, width):
    return pl.BlockSpec((None, tile, width), lambda b, i: (b, i, 0))


def _params():
    return pltpu.CompilerParams(
        dimension_semantics=("arbitrary", "arbitrary"), vmem_limit_bytes=VMEM_LIMIT)


def _causal_conv3(buf_ref, cur, w, tile):
    halo = V7X_SUBLANES
    y = buf_ref[pl.ds(halo - 2, tile), :] * w[0:1]
    y = y + buf_ref[pl.ds(halo - 1, tile), :] * w[1:2]
    return y + cur * w[2:3]


def _mixer_kernel(x_ref, g_ref, win_ref, cw_ref, wout_ref, o_ref, cu_ref):
    tile, d = x_ref.shape
    halo = V7X_SUBLANES

    @pl.when(pl.program_id(1) == 0)
    def _():
        cu_ref[0:halo, :] = jnp.zeros((halo, d), _F32)

    x = x_ref[...]
    hn = _rms(x, g_ref[...]).astype(_BF16)
    b_gate = _dot(hn, win_ref[:, 0:d])
    cu = _dot(hn, win_ref[:, d:2 * d]) * _dot(hn, win_ref[:, 2 * d:3 * d])
    cu_ref[halo:halo + tile, :] = cu
    conv = _causal_conv3(cu_ref, cu, cw_ref[...], tile)
    cu_ref[0:halo, :] = cu_ref[tile:tile + halo, :]
    y = (b_gate * conv).astype(_BF16)
    o_ref[...] = x + _dot(y, wout_ref[...])


def _mixer(x, g, w_in, conv_w, w_out):
    bsz, s, d = x.shape
    tile = ROW_TILE
    return pl.pallas_call(
        _mixer_kernel,
        grid=(bsz, s // tile),
        in_specs=[_row_spec(tile, d), _const_spec((1, d)), _const_spec((d, 3 * d)),
                  _const_spec((CONV_W, d)), _const_spec((d, d))],
        out_specs=_row_spec(tile, d),
        out_shape=jax.ShapeDtypeStruct(x.shape, _F32),
        scratch_shapes=[pltpu.VMEM((tile + V7X_SUBLANES, d), _F32)],
        compiler_params=_params(),
        name="sc_mixer",
    )(x, g, w_in, conv_w, w_out)


def _ffn_kernel(h_ref, g_ref, wup_ref, cw_ref, cb_ref, wdn_ref, fg_ref, o_ref,
                gbuf_ref, act_ref, *, final_norm):
    tile, d = h_ref.shape
    dff = act_ref.shape[1]
    halo = V7X_SUBLANES
    ch = FF_CHUNK

    @pl.when(pl.program_id(1) == 0)
    def _():
        gbuf_ref[0:halo, :] = jnp.zeros((halo, dff), _F32)

    h = h_ref[...]
    hn = _rms(h, g_ref[...]).astype(_BF16)
    for c in range(dff // ch):
        cols = slice(c * ch, (c + 1) * ch)
        g = _dot(hn, wup_ref[:, cols])
        v = _dot(hn, wup_ref[:, dff + c * ch:dff + (c + 1) * ch])
        gbuf_ref[halo:halo + tile, cols] = g
        pre = _causal_conv3(gbuf_ref.at[:, cols], g, cw_ref[:, cols], tile) + cb_ref[:, cols]
        gbuf_ref[0:halo, cols] = gbuf_ref[tile:tile + halo, cols]
        act = pre * (1.0 / (1.0 + jnp.exp(-pre))) * v
        act_ref[:, cols] = act.astype(_BF16)
    out = h + _dot(act_ref[...], wdn_ref[...])
    if final_norm:
        out = _rms(out, fg_ref[...])
    o_ref[...] = out


def _ffn(h, g, w_up, conv_w, conv_b, w_down, final_g, *, final_norm):
    bsz, s, d = h.shape
    dff = w_down.shape[0]
    tile = ROW_TILE
    return pl.pallas_call(
        functools.partial(_ffn_kernel, final_norm=final_norm),
        grid=(bsz, s // tile),
        in_specs=[_row_spec(tile, d), _const_spec((1, d)), _const_spec((d, 2 * dff)),
                  _const_spec((CONV_W, dff)), _const_spec((1, dff)), _const_spec((dff, d)),
                  _const_spec((1, d))],
        out_specs=_row_spec(tile, d),
        out_shape=jax.ShapeDtypeStruct(h.shape, _F32),
        scratch_shapes=[pltpu.VMEM((tile + V7X_SUBLANES, dff), _F32),
                        pltpu.VMEM((tile, dff), _BF16)],
        compiler_params=_params(),
        name="conv_ffn_final" if final_norm else "conv_ffn",
    )(h, g, w_up, conv_w, conv_b, w_down, final_g)


def _rope_table_t(pos_row, inv_freq_col, scale):
    ang = pos_row.astype(_F32) * inv_freq_col
    cos, sin = jnp.cos(ang), jnp.sin(ang)
    table = jnp.concatenate([cos, cos, -sin, sin], axis=0)
    return table * scale if scale != 1.0 else table


def _kv_kernel(h_ref, pos_ref, freq_ref, gin_ref, wdkv_ref, glat_ref, wkr_ref, wuk_ref,
               wuvt_ref, kn_ref, kr_ref, vt_ref):
    hn = _rms(h_ref[...], gin_ref[...]).astype(_BF16)
    c_kv = _rms(_dot(hn, wdkv_ref[...]), glat_ref[...])
    p = _dot(hn, wkr_ref[...]) * _rope_table_t(pos_ref[...], freq_ref[...], 1.0).T
    kr_ref[...] = (p + pltpu.roll(p, QK_ROPE, 1)).astype(_BF16)
    kn_ref[...] = _dot(c_kv.astype(_BF16), wuk_ref[...]).astype(_BF16)
    vt_ref[...] = _dot(wuvt_ref[...], c_kv.T.astype(_BF16)).astype(_BF16)


def _shared_kv(h, pos, freq, g_in, w_dkv, g_lat, w_kr2, w_uk, w_uv_t):
    bsz, s, d = h.shape
    lora = w_dkv.shape[1]
    hk = w_uk.shape[1]
    hv = w_uv_t.shape[0]
    tile = ROW_TILE
    return pl.pallas_call(
        _kv_kernel,
        grid=(bsz, s // tile),
        in_specs=[_row_spec(tile, d), pl.BlockSpec((None, 1, tile), lambda b, i: (b, 0, i)),
                  _const_spec((QK_ROPE // 2, 1)),
                  _const_spec((1, d)), _const_spec((d, lora)), _const_spec((1, lora)),
                  _const_spec((d, 2 * QK_ROPE)), _const_spec((lora, hk)),
                  _const_spec((hv, lora))],
        out_specs=[_row_spec(tile, hk), _row_spec(tile, 2 * QK_ROPE),
                   pl.BlockSpec((None, hv, tile), lambda b, i: (b, 0, i))],
        out_shape=[jax.ShapeDtypeStruct((bsz, s, hk), _BF16),
                   jax.ShapeDtypeStruct((bsz, s, 2 * QK_ROPE), _BF16),
                   jax.ShapeDtypeStruct((bsz, hv, s), _BF16)],
        compiler_params=_params(),
        name="shared_kv",
    )(h, pos, freq, g_in, w_dkv, g_lat, w_kr2, w_uk, w_uv_t)


def _mla_kernel(h_ref, posr_ref, freqc_ref, g_ref, wdq_ref, gq_ref, wuqt_ref, wo_ref,
                kn_ref, kr_ref, vt_ref, o_ref, qt_ref, ob_ref):
    tq, d = h_ref.shape
    i = pl.program_id(1)
    scale = float(QK_NOPE + QK_ROPE) ** -0.5 * math.log2(math.e)

    h = h_ref[...]
    hn = _rms(h, g_ref[...]).astype(_BF16)
    c_q = _rms(_dot(hn, wdq_ref[...]), gq_ref[...])
    c_qt = c_q.T.astype(_BF16)
    table = _rope_table_t(posr_ref[...], freqc_ref[...], scale)
    for hd in range(N_HEADS):
        qt = _dot(wuqt_ref[hd * HEAD_W:(hd + 1) * HEAD_W, :], c_qt)
        qt_ref[hd * HEAD_W:hd * HEAD_W + QK_NOPE, :] = (qt[0:QK_NOPE] * scale).astype(_BF16)
        qt_ref[hd * HEAD_W + QK_NOPE:(hd + 1) * HEAD_W, :] = (qt[QK_NOPE:] * table).astype(_BF16)

    ones_rows = 2 * V7X_SUBLANES
    k_chunk = lax.broadcasted_iota(jnp.int32, (tq, tq), 0) // CHUNK
    q_chunk = lax.broadcasted_iota(jnp.int32, (tq, tq), 1) // CHUNK
    diag_mask = k_chunk <= q_chunk

    def attend(n_keys):
        ones = jnp.ones((ones_rows, n_keys), _BF16)
        k_rope = kr_ref[0:n_keys, :]

        def scores(hd):
            k = jnp.concatenate([kn_ref[0:n_keys, hd * QK_NOPE:(hd + 1) * QK_NOPE], k_rope], axis=1)
            return _dot(k, qt_ref[hd * HEAD_W:(hd + 1) * HEAD_W, :])

        s_next = scores(0)
        for hd in range(N_HEADS):
            s = s_next
            if hd + 1 < N_HEADS:
                s_next = scores(hd + 1)
            s_diag = jnp.where(diag_mask, s[n_keys - tq:], NEG_INF)
            s = jnp.concatenate([s[:n_keys - tq], s_diag], axis=0) if n_keys > tq else s_diag
            p = jnp.exp2(s - jnp.max(s, axis=0, keepdims=True)).astype(_BF16)
            v_aug = jnp.concatenate([vt_ref[hd * V_HEAD:(hd + 1) * V_HEAD, 0:n_keys], ones], axis=0)
            o_aug = _dot(v_aug, p)
            o_t = o_aug[0:V_HEAD] * (1.0 / o_aug[V_HEAD:V_HEAD + 1])
            ob_ref[:, hd * V_HEAD:(hd + 1) * V_HEAD] = o_t.T.astype(_BF16)

    for c in range(kn_ref.shape[0] // tq):
        pl.when(i == c)(functools.partial(attend, (c + 1) * tq))

    o_ref[...] = h + _dot(ob_ref[...], wo_ref[...])


def _mla(h, pos_row, freq_col, g, w_dq, g_q, w_uq2_t, w_o, k_nope, k_rope, v_t):
    bsz, s, d = h.shape
    qlora = w_dq.shape[1]
    hk = k_nope.shape[2]
    hv = v_t.shape[1]
    tq = Q_TILE
    return pl.pallas_call(
        _mla_kernel,
        grid=(bsz, s // tq),
        in_specs=[_row_spec(tq, d), pl.BlockSpec((None, 1, tq), lambda b, i: (b, 0, i)),
                  _const_spec((QK_ROPE // 2, 1)),
                  _const_spec((1, d)), _const_spec((d, qlora)), _const_spec((1, qlora)),
                  _const_spec((N_HEADS * HEAD_W, qlora)), _const_spec((hv, d)),
                  pl.BlockSpec((None, s, hk), lambda b, i: (b, 0, 0)),
                  pl.BlockSpec((None, s, 2 * QK_ROPE), lambda b, i: (b, 0, 0)),
                  pl.BlockSpec((None, hv, s), lambda b, i: (b, 0, 0))],
        out_specs=_row_spec(tq, d),
        out_shape=jax.ShapeDtypeStruct(h.shape, _F32),
        scratch_shapes=[pltpu.VMEM((N_HEADS * HEAD_W, tq), _BF16),
                        pltpu.VMEM((tq, hv), _BF16)],
        compiler_params=_params(),
        name="mla_attention",
    )(h, pos_row, freq_col, g, w_dq, g_q, w_uq2_t, w_o, k_nope, k_rope, v_t)


def _swap_halves(w):
    half = w.shape[-1] // 2
    return jnp.concatenate([w[..., half:], w[..., :half]], axis=-1)


def kernel(x, positions, attn_norm, ffn_norm, final_norm, sc_w_in, sc_conv_w, sc_w_out, kv_in_norm, w_dkv, kv_latent_norm, w_kr, w_uk, w_uv, w_dq, q_latent_norm, w_uq, w_o, ffn_w_up, ffn_conv_w, ffn_conv_b, ffn_w_down):
    depth = attn_norm.shape[0]
    n_a = sc_w_in.shape[0]
    row = lambda v: v.reshape(1, -1)
    bf = lambda w: w.astype(_BF16)

    half = QK_ROPE // 2
    inv_freq = 1.0 / (ROPE_THETA ** (jnp.arange(half, dtype=_F32) / half))
    freq_col = inv_freq.reshape(half, 1)
    pos_row = positions.reshape(positions.shape[0], 1, positions.shape[1])

    w_kr2 = bf(jnp.concatenate([w_kr, _swap_halves(w_kr)], axis=1))
    final_g = row(final_norm)

    h = x
    kv = None
    for layer in range(depth):
        g_attn = row(attn_norm[layer])
        if layer < n_a:
            h = _mixer(h, g_attn, bf(sc_w_in[layer]), sc_conv_w[layer], bf(sc_w_out[layer]))
        else:
            a = layer - n_a
            wq = w_uq[a].reshape(-1, N_HEADS, QK_NOPE + QK_ROPE)
            rope_cols = wq[:, :, QK_NOPE:]
            w_uq2 = jnp.concatenate([wq, _swap_halves(rope_cols)], axis=2)
            w_uq2_t = bf(w_uq2.reshape(-1, N_HEADS * HEAD_W).T)
            h = _mla(h, pos_row, freq_col, g_attn, bf(w_dq[a]), row(q_latent_norm[a]), w_uq2_t,
                     bf(w_o[a]), *kv)
        h = _ffn(h, row(ffn_norm[layer]), bf(ffn_w_up[layer]),
                 ffn_conv_w[layer], row(ffn_conv_b[layer]), bf(ffn_w_down[layer]), final_g,
                 final_norm=(layer == depth - 1))
        if layer == n_a - 1:
            kv = _shared_kv(h, pos_row, freq_col, row(kv_in_norm), bf(w_dkv), row(kv_latent_norm),
                            w_kr2, bf(w_uk), bf(w_uv.T))
    return h
```

```python
import functools
import math

import jax
import jax.numpy as jnp
from jax import lax
from jax.experimental import pallas as pl
from jax.experimental.pallas import tpu as pltpu

CHUNK = 64
CONV_W = 3
N_HEADS = 8
QK_NOPE = 128
QK_ROPE = 64
V_HEAD = 128
ROPE_THETA = 10000.0
EPS = 1e-6
NEG_INF = -1e30

V7X_SUBLANES = 8
V7X_LANES = 128
V7X_MXU_DIM = 256
V7X_VMEM_BYTES = 64 * 1024 * 1024

HEAD_W = QK_NOPE + 2 * QK_ROPE
ROW_TILE = 512
Q_TILE = 256
FF_CHUNK = V7X_MXU_DIM
VMEM_LIMIT = V7X_VMEM_BYTES - 8 * 1024 * 1024

_BF16 = jnp.bfloat16
_F32 = jnp.float32


def _dot(a, b):
    return jnp.dot(a, b, preferred_element_type=_F32)


def _rms(x, g):
    return x * lax.rsqrt(jnp.mean(x * x, axis=-1, keepdims=True) + EPS) * g


def _const_spec(shape):
    nd = len(shape)
    return pl.BlockSpec(shape, lambda *_: (0,) * nd, pipeline_mode=pl.Buffered(1))


def _row_spec(tile, width):
    return pl.BlockSpec((None, tile, width), lambda b, i: (b, i, 0))


def _params():
    return pltpu.CompilerParams(
        dimension_semantics=("arbitrary", "arbitrary"), vmem_limit_bytes=VMEM_LIMIT)


def _causal_conv3(buf_ref, cur, w, tile):
    halo = V7X_SUBLANES
    y = buf_ref[pl.ds(halo - 2, tile), :] * w[0:1]
    y = y + buf_ref[pl.ds(halo - 1, tile), :] * w[1:2]
    return y + cur * w[2:3]


def _mixer_kernel(x_ref, g_ref, win_ref, cw_ref, wout_ref, o_ref, cu_ref):
    tile, d = x_ref.shape
    halo = V7X_SUBLANES

    @pl.when(pl.program_id(1) == 0)
    def _():
        cu_ref[0:halo, :] = jnp.zeros((halo, d), _F32)

    x = x_ref[...]
    hn = _rms(x, g_ref[...]).astype(_BF16)
    b_gate = _dot(hn, win_ref[:, 0:d])
    cu = _dot(hn, win_ref[:, d:2 * d]) * _dot(hn, win_ref[:, 2 * d:3 * d])
    cu_ref[halo:halo + tile, :] = cu
    conv = _causal_conv3(cu_ref, cu, cw_ref[...], tile)
    cu_ref[0:halo, :] = cu_ref[tile:tile + halo, :]
    y = (b_gate * conv).astype(_BF16)
    o_ref[...] = x + _dot(y, wout_ref[...])


def _mixer(x, g, w_in, conv_w, w_out):
    bsz, s, d = x.shape
    tile = ROW_TILE
    return pl.pallas_call(
        _mixer_kernel,
        grid=(bsz, s // tile),
        in_specs=[_row_spec(tile, d), _const_spec((1, d)), _const_spec((d, 3 * d)),
                  _const_spec((CONV_W, d)), _const_spec((d, d))],
        out_specs=_row_spec(tile, d),
        out_shape=jax.ShapeDtypeStruct(x.shape, _F32),
        scratch_shapes=[pltpu.VMEM((tile + V7X_SUBLANES, d), _F32)],
        compiler_params=_params(),
        name="sc_mixer",
    )(x, g, w_in, conv_w, w_out)


def _ffn_kernel(h_ref, g_ref, wup_ref, cw_ref, cb_ref, wdn_ref, fg_ref, o_ref,
                gbuf_ref, act_ref, *, final_norm):
    tile, d = h_ref.shape
    dff = act_ref.shape[1]
    halo = V7X_SUBLANES
    ch = FF_CHUNK

    @pl.when(pl.program_id(1) == 0)
    def _():
        gbuf_ref[0:halo, :] = jnp.zeros((halo, dff), _F32)

    h = h_ref[...]
    hn = _rms(h, g_ref[...]).astype(_BF16)
    for c in range(dff // ch):
        cols = slice(c * ch, (c + 1) * ch)
        g = _dot(hn, wup_ref[:, cols])
        v = _dot(hn, wup_ref[:, dff + c * ch:dff + (c + 1) * ch])
        gbuf_ref[halo:halo + tile, cols] = g
        pre = _causal_conv3(gbuf_ref.at[:, cols], g, cw_ref[:, cols], tile) + cb_ref[:, cols]
        gbuf_ref[0:halo, cols] = gbuf_ref[tile:tile + halo, cols]
        act = pre * (1.0 / (1.0 + jnp.exp(-pre))) * v
        act_ref[:, cols] = act.astype(_BF16)
    out = h + _dot(act_ref[...], wdn_ref[...])
    if final_norm:
        out = _rms(out, fg_ref[...])
    o_ref[...] = out


def _ffn(h, g, w_up, conv_w, conv_b, w_down, final_g, *, final_norm):
    bsz, s, d = h.shape
    dff = w_down.shape[0]
    tile = ROW_TILE
    return pl.pallas_call(
        functools.partial(_ffn_kernel, final_norm=final_norm),
        grid=(bsz, s // tile),
        in_specs=[_row_spec(tile, d), _const_spec((1, d)), _const_spec((d, 2 * dff)),
                  _const_spec((CONV_W, dff)), _const_spec((1, dff)), _const_spec((dff, d)),
                  _const_spec((1, d))],
        out_specs=_row_spec(tile, d),
        out_shape=jax.ShapeDtypeStruct(h.shape, _F32),
        scratch_shapes=[pltpu.VMEM((tile + V7X_SUBLANES, dff), _F32),
                        pltpu.VMEM((tile, dff), _BF16)],
        compiler_params=_params(),
        name="conv_ffn_final" if final_norm else "conv_ffn",
    )(h, g, w_up, conv_w, conv_b, w_down, final_g)


def _rope_table_t(pos_row, inv_freq_col, scale):
    ang = pos_row.astype(_F32) * inv_freq_col
    cos, sin = jnp.cos(ang), jnp.sin(ang)
    table = jnp.concatenate([cos, cos, -sin, sin], axis=0)
    return table * scale if scale != 1.0 else table


def _kv_kernel(h_ref, pos_ref, freq_ref, gin_ref, wdkv_ref, glat_ref, wkr_ref, wuk_ref,
               wuvt_ref, kn_ref, kr_ref, vt_ref):
    hn = _rms(h_ref[...], gin_ref[...]).astype(_BF16)
    c_kv = _rms(_dot(hn, wdkv_ref[...]), glat_ref[...])
    p = _dot(hn, wkr_ref[...]) * _rope_table_t(pos_ref[...], freq_ref[...], 1.0).T
    kr_ref[...] = (p + pltpu.roll(p, QK_ROPE, 1)).astype(_BF16)
    kn_ref[...] = _dot(c_kv.astype(_BF16), wuk_ref[...]).astype(_BF16)
    vt_ref[...] = _dot(wuvt_ref[...], c_kv.T.astype(_BF16)).astype(_BF16)


def _shared_kv(h, pos, freq, g_in, w_dkv, g_lat, w_kr2, w_uk, w_uv_t):
    bsz, s, d = h.shape
    lora = w_dkv.shape[1]
    hk = w_uk.shape[1]
    hv = w_uv_t.shape[0]
    tile = ROW_TILE
    return pl.pallas_call(
        _kv_kernel,
        grid=(bsz, s // tile),
        in_specs=[_row_spec(tile, d), pl.BlockSpec((None, 1, tile), lambda b, i: (b, 0, i)),
                  _const_spec((QK_ROPE // 2, 1)),
                  _const_spec((1, d)), _const_spec((d, lora)), _const_spec((1, lora)),
                  _const_spec((d, 2 * QK_ROPE)), _const_spec((lora, hk)),
                  _const_spec((hv, lora))],
        out_specs=[_row_spec(tile, hk), _row_spec(tile, 2 * QK_ROPE),
                   pl.BlockSpec((None, hv, tile), lambda b, i: (b, 0, i))],
        out_shape=[jax.ShapeDtypeStruct((bsz, s, hk), _BF16),
                   jax.ShapeDtypeStruct((bsz, s, 2 * QK_ROPE), _BF16),
                   jax.ShapeDtypeStruct((bsz, hv, s), _BF16)],
        compiler_params=_params(),
        name="shared_kv",
    )(h, pos, freq, g_in, w_dkv, g_lat, w_kr2, w_uk, w_uv_t)


def _mla_kernel(h_ref, posr_ref, freqc_ref, g_ref, wdq_ref, gq_ref, wuqt_ref, wo_ref,
                kn_ref, kr_ref, vt_ref, o_ref, qt_ref, ob_ref, s_ref):
    tq, d = h_ref.shape
    i = pl.program_id(1)
    scale = float(QK_NOPE + QK_ROPE) ** -0.5 * math.log2(math.e)

    h = h_ref[...]
    hn = _rms(h, g_ref[...]).astype(_BF16)
    c_q = _rms(_dot(hn, wdq_ref[...]), gq_ref[...])
    c_qt = c_q.T.astype(_BF16)
    table = _rope_table_t(posr_ref[...], freqc_ref[...], scale)
    for hd in range(N_HEADS):
        qt = _dot(wuqt_ref[hd * HEAD_W:(hd + 1) * HEAD_W, :], c_qt)
        qt_ref[hd * HEAD_W:hd * HEAD_W + QK_NOPE, :] = (qt[0:QK_NOPE] * scale).astype(_BF16)
        qt_ref[hd * HEAD_W + QK_NOPE:(hd + 1) * HEAD_W, :] = (qt[QK_NOPE:] * table).astype(_BF16)

    ones_rows = 2 * V7X_SUBLANES
    k_chunk = lax.broadcasted_iota(jnp.int32, (tq, tq), 0) // CHUNK
    q_chunk = lax.broadcasted_iota(jnp.int32, (tq, tq), 1) // CHUNK
    diag_mask = k_chunk <= q_chunk

    ones = jnp.ones((ones_rows, tq), _BF16)

    def attend(n_blocks):
        def score_block(hd, t, m8):
            rows = slice(t * tq, (t + 1) * tq)
            k = jnp.concatenate([kn_ref[rows, hd * QK_NOPE:(hd + 1) * QK_NOPE], kr_ref[rows, :]],
                                axis=1)
            s = _dot(k, qt_ref[hd * HEAD_W:(hd + 1) * HEAD_W, :])
            if t == n_blocks - 1:
                s = jnp.where(diag_mask, s, NEG_INF)
            s_ref[hd % 2, rows, :] = s
            blk_max = jnp.max(s.reshape(tq // V7X_SUBLANES, V7X_SUBLANES, tq), axis=0)
            return blk_max if m8 is None else jnp.maximum(m8, blk_max)

        def pv_block(hd, t, m, acc):
            rows = slice(t * tq, (t + 1) * tq)
            p = jnp.exp2(s_ref[hd % 2, rows, :] - m).astype(_BF16)
            v_aug = jnp.concatenate([vt_ref[hd * V_HEAD:(hd + 1) * V_HEAD, rows], ones], axis=0)
            o_aug = _dot(v_aug, p)
            return o_aug if acc is None else acc + o_aug

        m8_next = None
        for t in range(n_blocks):
            m8_next = score_block(0, t, m8_next)
        for hd in range(N_HEADS):
            m = jnp.max(m8_next, axis=0, keepdims=True)
            m8_next, acc = None, None
            for t in range(n_blocks):
                if hd + 1 < N_HEADS:
                    m8_next = score_block(hd + 1, t, m8_next)
                acc = pv_block(hd, t, m, acc)
            o_t = acc[0:V_HEAD] * (1.0 / acc[V_HEAD:V_HEAD + 1])
            ob_ref[:, hd * V_HEAD:(hd + 1) * V_HEAD] = o_t.T.astype(_BF16)

    for c in range(kn_ref.shape[0] // tq):
        pl.when(i == c)(functools.partial(attend, c + 1))

    o_ref[...] = h + _dot(ob_ref[...], wo_ref[...])


def _mla(h, pos_row, freq_col, g, w_dq, g_q, w_uq2_t, w_o, k_nope, k_rope, v_t):
    bsz, s, d = h.shape
    qlora = w_dq.shape[1]
    hk = k_nope.shape[2]
    hv = v_t.shape[1]
    tq = Q_TILE
    return pl.pallas_call(
        _mla_kernel,
        grid=(bsz, s // tq),
        in_specs=[_row_spec(tq, d), pl.BlockSpec((None, 1, tq), lambda b, i: (b, 0, i)),
                  _const_spec((QK_ROPE // 2, 1)),
                  _const_spec((1, d)), _const_spec((d, qlora)), _const_spec((1, qlora)),
                  _const_spec((N_HEADS * HEAD_W, qlora)), _const_spec((hv, d)),
                  pl.BlockSpec((None, s, hk), lambda b, i: (b, 0, 0)),
                  pl.BlockSpec((None, s, 2 * QK_ROPE), lambda b, i: (b, 0, 0)),
                  pl.BlockSpec((None, hv, s), lambda b, i: (b, 0, 0))],
        out_specs=_row_spec(tq, d),
        out_shape=jax.ShapeDtypeStruct(h.shape, _F32),
        scratch_shapes=[pltpu.VMEM((N_HEADS * HEAD_W, tq), _BF16),
                        pltpu.VMEM((tq, hv), _BF16),
                        pltpu.VMEM((2, s, tq), _F32)],
        compiler_params=_params(),
        name="mla_attention",
    )(h, pos_row, freq_col, g, w_dq, g_q, w_uq2_t, w_o, k_nope, k_rope, v_t)


def _swap_halves(w):
    half = w.shape[-1] // 2
    return jnp.concatenate([w[..., half:], w[..., :half]], axis=-1)


def kernel(x, positions, attn_norm, ffn_norm, final_norm, sc_w_in, sc_conv_w, sc_w_out, kv_in_norm, w_dkv, kv_latent_norm, w_kr, w_uk, w_uv, w_dq, q_latent_norm, w_uq, w_o, ffn_w_up, ffn_conv_w, ffn_conv_b, ffn_w_down):
    depth = attn_norm.shape[0]
    n_a = sc_w_in.shape[0]
    row = lambda v: v.reshape(1, -1)
    bf = lambda w: w.astype(_BF16)

    half = QK_ROPE // 2
    inv_freq = 1.0 / (ROPE_THETA ** (jnp.arange(half, dtype=_F32) / half))
    freq_col = inv_freq.reshape(half, 1)
    pos_row = positions.reshape(positions.shape[0], 1, positions.shape[1])

    w_kr2 = bf(jnp.concatenate([w_kr, _swap_halves(w_kr)], axis=1))
    final_g = row(final_norm)

    h = x
    kv = None
    for layer in range(depth):
        g_attn = row(attn_norm[layer])
        if layer < n_a:
            h = _mixer(h, g_attn, bf(sc_w_in[layer]), sc_conv_w[layer], bf(sc_w_out[layer]))
        else:
            a = layer - n_a
            wq = w_uq[a].reshape(-1, N_HEADS, QK_NOPE + QK_ROPE)
            rope_cols = wq[:, :, QK_NOPE:]
            w_uq2 = jnp.concatenate([wq, _swap_halves(rope_cols)], axis=2)
            w_uq2_t = bf(w_uq2.reshape(-1, N_HEADS * HEAD_W).T)
            h = _mla(h, pos_row, freq_col, g_attn, bf(w_dq[a]), row(q_latent_norm[a]), w_uq2_t,
                     bf(w_o[a]), *kv)
        h = _ffn(h, row(ffn_norm[layer]), bf(ffn_w_up[layer]),
                 ffn_conv_w[layer], row(ffn_conv_b[layer]), bf(ffn_w_down[layer]), final_g,
                 final_norm=(layer == depth - 1))
        if layer == n_a - 1:
            kv = _shared_kv(h, pos_row, freq_col, row(kv_in_norm), bf(w_dkv), row(kv_latent_norm),
                            w_kr2, bf(w_uk), bf(w_uv.T))
    return h
```

```python
import functools
import math

import jax
import jax.numpy as jnp
from jax import lax
from jax.experimental import pallas as pl
from jax.experimental.pallas import tpu as pltpu

CHUNK = 64
CONV_W = 3
N_HEADS = 8
QK_NOPE = 128
QK_ROPE = 64
V_HEAD = 128
ROPE_THETA = 10000.0
EPS = 1e-6
NEG_INF = -1e30

V7X_SUBLANES = 8
V7X_LANES = 128
V7X_MXU_DIM = 256
V7X_VMEM_BYTES = 64 * 1024 * 1024

HEAD_W = QK_NOPE + 2 * QK_ROPE
ROW_TILE = 512
Q_TILE = 256
FF_CHUNK = V7X_MXU_DIM
VMEM_LIMIT = V7X_VMEM_BYTES - 8 * 1024 * 1024

_BF16 = jnp.bfloat16
_F32 = jnp.float32


def _dot(a, b):
    return jnp.dot(a, b, preferred_element_type=_F32)


def _rms(x, g):
    return x * lax.rsqrt(jnp.mean(x * x, axis=-1, keepdims=True) + EPS) * g


def _const_spec(shape):
    nd = len(shape)
    return pl.BlockSpec(shape, lambda *_: (0,) * nd, pipeline_mode=pl.Buffered(1))


def _layer_spec(shape, layer):
    nd = len(shape)
    return pl.BlockSpec((None,) + tuple(shape), lambda *_: (layer,) + (0,) * nd,
                        pipeline_mode=pl.Buffered(1))


def _row_spec(tile, width):
    return pl.BlockSpec((None, tile, width), lambda b, i: (b, i, 0))


def _params():
    return pltpu.CompilerParams(
        dimension_semantics=("arbitrary", "arbitrary"), vmem_limit_bytes=VMEM_LIMIT)


def _causal_conv3(buf_ref, cur, w, tile):
    halo = V7X_SUBLANES
    y = buf_ref[pl.ds(halo - 2, tile), :] * w[0:1]
    y = y + buf_ref[pl.ds(halo - 1, tile), :] * w[1:2]
    return y + cur * w[2:3]


def _mixer_kernel(x_ref, g_ref, win_ref, cw_ref, wout_ref, o_ref, cu_ref):
    tile, d = x_ref.shape
    halo = V7X_SUBLANES

    @pl.when(pl.program_id(1) == 0)
    def _():
        cu_ref[0:halo, :] = jnp.zeros((halo, d), _F32)

    x = x_ref[...]
    hn = _rms(x, g_ref[...])
    b_gate = _dot(hn, win_ref[:, 0:d])
    cu = _dot(hn, win_ref[:, d:2 * d]) * _dot(hn, win_ref[:, 2 * d:3 * d])
    cu_ref[halo:halo + tile, :] = cu
    conv = _causal_conv3(cu_ref, cu, cw_ref[...], tile)
    cu_ref[0:halo, :] = cu_ref[tile:tile + halo, :]
    o_ref[...] = x + _dot(b_gate * conv, wout_ref[...])


def _mixer(x, g, w_in, conv_w, w_out):
    bsz, s, d = x.shape
    tile = ROW_TILE
    return pl.pallas_call(
        _mixer_kernel,
        grid=(bsz, s // tile),
        in_specs=[_row_spec(tile, d), _const_spec((1, d)), _const_spec((d, 3 * d)),
                  _const_spec((CONV_W, d)), _const_spec((d, d))],
        out_specs=_row_spec(tile, d),
        out_shape=jax.ShapeDtypeStruct(x.shape, _F32),
        scratch_shapes=[pltpu.VMEM((tile + V7X_SUBLANES, d), _F32)],
        compiler_params=_params(),
        name="sc_mixer",
    )(x, g, w_in, conv_w, w_out)


def _ffn_kernel(h_ref, g_ref, wup_ref, cw_ref, cb_ref, wdn_ref, fg_ref, o_ref,
                gbuf_ref, act_ref, *, final_norm):
    tile, d = h_ref.shape
    dff = act_ref.shape[1]
    halo = V7X_SUBLANES
    ch = FF_CHUNK

    @pl.when(pl.program_id(1) == 0)
    def _():
        gbuf_ref[0:halo, :] = jnp.zeros((halo, dff), _F32)

    h = h_ref[...]
    hn = _rms(h, g_ref[...]).astype(_BF16)
    for c in range(dff // ch):
        cols = slice(c * ch, (c + 1) * ch)
        g = _dot(hn, wup_ref[:, cols])
        v = _dot(hn, wup_ref[:, dff + c * ch:dff + (c + 1) * ch])
        gbuf_ref[halo:halo + tile, cols] = g
        pre = _causal_conv3(gbuf_ref.at[:, cols], g, cw_ref[:, cols], tile) + cb_ref[:, cols]
        gbuf_ref[0:halo, cols] = gbuf_ref[tile:tile + halo, cols]
        act = pre * (1.0 / (1.0 + jnp.exp(-pre))) * v
        act_ref[:, cols] = act.astype(_BF16)
    out = h + _dot(act_ref[...], wdn_ref[...])
    if final_norm:
        out = _rms(out, fg_ref[...])
    o_ref[...] = out


def _ffn(h, g, w_up, conv_w, conv_b, w_down, final_g, *, layer, final_norm):
    bsz, s, d = h.shape
    dff = w_down.shape[1]
    tile = ROW_TILE
    return pl.pallas_call(
        functools.partial(_ffn_kernel, final_norm=final_norm),
        grid=(bsz, s // tile),
        in_specs=[_row_spec(tile, d), _const_spec((1, d)), _layer_spec((d, 2 * dff), layer),
                  _const_spec((CONV_W, dff)), _const_spec((1, dff)), _layer_spec((dff, d), layer),
                  _const_spec((1, d))],
        out_specs=_row_spec(tile, d),
        out_shape=jax.ShapeDtypeStruct(h.shape, _F32),
        scratch_shapes=[pltpu.VMEM((tile + V7X_SUBLANES, dff), _F32),
                        pltpu.VMEM((tile, dff), _BF16)],
        compiler_params=_params(),
        name="conv_ffn_final" if final_norm else "conv_ffn",
    )(h, g, w_up, conv_w, conv_b, w_down, final_g)


def _rope_table_t(pos_row, inv_freq_col, scale):
    ang = pos_row.astype(_F32) * inv_freq_col
    cos, sin = jnp.cos(ang), jnp.sin(ang)
    table = jnp.concatenate([cos, cos, -sin, sin], axis=0)
    return table * scale if scale != 1.0 else table


def _kv_kernel(h_ref, pos_ref, freq_ref, gin_ref, wdkv_ref, glat_ref, wkr_ref, wuk_ref,
               wuvt_ref, kn_ref, kr_ref, vt_ref):
    hn = _rms(h_ref[...], gin_ref[...])
    c_kv = _rms(_dot(hn, wdkv_ref[...]), glat_ref[...])
    p = _dot(hn, wkr_ref[...]) * _rope_table_t(pos_ref[...], freq_ref[...], 1.0).T
    kr_ref[...] = (p + pltpu.roll(p, QK_ROPE, 1)).astype(_BF16)
    kn_ref[...] = _dot(c_kv, wuk_ref[...]).astype(_BF16)
    vt_ref[...] = _dot(wuvt_ref[...], c_kv.T).astype(_BF16)


def _shared_kv(h, pos, freq, g_in, w_dkv, g_lat, w_kr2, w_uk, w_uv_t):
    bsz, s, d = h.shape
    lora = w_dkv.shape[1]
    hk = w_uk.shape[1]
    hv = w_uv_t.shape[0]
    tile = ROW_TILE
    return pl.pallas_call(
        _kv_kernel,
        grid=(bsz, s // tile),
        in_specs=[_row_spec(tile, d), pl.BlockSpec((None, 1, tile), lambda b, i: (b, 0, i)),
                  _const_spec((QK_ROPE // 2, 1)),
                  _const_spec((1, d)), _const_spec((d, lora)), _const_spec((1, lora)),
                  _const_spec((d, 2 * QK_ROPE)), _const_spec((lora, hk)),
                  _const_spec((hv, lora))],
        out_specs=[_row_spec(tile, hk), _row_spec(tile, 2 * QK_ROPE),
                   pl.BlockSpec((None, hv, tile), lambda b, i: (b, 0, i))],
        out_shape=[jax.ShapeDtypeStruct((bsz, s, hk), _BF16),
                   jax.ShapeDtypeStruct((bsz, s, 2 * QK_ROPE), _BF16),
                   jax.ShapeDtypeStruct((bsz, hv, s), _BF16)],
        compiler_params=_params(),
        name="shared_kv",
    )(h, pos, freq, g_in, w_dkv, g_lat, w_kr2, w_uk, w_uv_t)


def _mla_kernel(h_ref, posr_ref, freqc_ref, g_ref, wdq_ref, gq_ref, wuqt_ref, wo_ref,
                kn_ref, kr_ref, vt_ref, o_ref, qt_ref, ob_ref, s_ref):
    tq, d = h_ref.shape
    i = pl.program_id(1)
    scale = float(QK_NOPE + QK_ROPE) ** -0.5 * math.log2(math.e)

    h = h_ref[...]
    hn = _rms(h, g_ref[...])
    c_q = _rms(_dot(hn, wdq_ref[...]), gq_ref[...])
    c_qt = c_q.T
    table = _rope_table_t(posr_ref[...], freqc_ref[...], scale)
    for hd in range(N_HEADS):
        qt = _dot(wuqt_ref[hd * HEAD_W:(hd + 1) * HEAD_W, :], c_qt)
        qt_ref[hd * HEAD_W:hd * HEAD_W + QK_NOPE, :] = (qt[0:QK_NOPE] * scale).astype(_BF16)
        qt_ref[hd * HEAD_W + QK_NOPE:(hd + 1) * HEAD_W, :] = (qt[QK_NOPE:] * table).astype(_BF16)

    ones_rows = 2 * V7X_SUBLANES
    k_chunk = lax.broadcasted_iota(jnp.int32, (tq, tq), 0) // CHUNK
    q_chunk = lax.broadcasted_iota(jnp.int32, (tq, tq), 1) // CHUNK
    diag_mask = k_chunk <= q_chunk

    ones = jnp.ones((ones_rows, tq), _BF16)

    def attend(n_blocks):
        def score_block(hd, t, m8):
            rows = slice(t * tq, (t + 1) * tq)
            k = jnp.concatenate([kn_ref[rows, hd * QK_NOPE:(hd + 1) * QK_NOPE], kr_ref[rows, :]],
                                axis=1)
            s = _dot(k, qt_ref[hd * HEAD_W:(hd + 1) * HEAD_W, :])
            if t == n_blocks - 1:
                s = jnp.where(diag_mask, s, NEG_INF)
            s_ref[hd % 2, rows, :] = s
            blk_max = jnp.max(s.reshape(tq // V7X_SUBLANES, V7X_SUBLANES, tq), axis=0)
            return blk_max if m8 is None else jnp.maximum(m8, blk_max)

        def pv_block(hd, t, m, acc):
            rows = slice(t * tq, (t + 1) * tq)
            p = jnp.exp2(s_ref[hd % 2, rows, :] - m).astype(_BF16)
            v_aug = jnp.concatenate([vt_ref[hd * V_HEAD:(hd + 1) * V_HEAD, rows], ones], axis=0)
            o_aug = _dot(v_aug, p)
            return o_aug if acc is None else acc + o_aug

        m8_next = None
        for t in range(n_blocks):
            m8_next = score_block(0, t, m8_next)
        for hd in range(N_HEADS):
            m = jnp.max(m8_next, axis=0, keepdims=True)
            m8_next, acc = None, None
            for t in range(n_blocks):
                if hd + 1 < N_HEADS:
                    m8_next = score_block(hd + 1, t, m8_next)
                acc = pv_block(hd, t, m, acc)
            o_t = acc[0:V_HEAD] * (1.0 / acc[V_HEAD:V_HEAD + 1])
            ob_ref[:, hd * V_HEAD:(hd + 1) * V_HEAD] = o_t.T

    for c in range(kn_ref.shape[0] // tq):
        pl.when(i == c)(functools.partial(attend, c + 1))

    o_ref[...] = h + _dot(ob_ref[...], wo_ref[...])


def _mla(h, pos_row, freq_col, g, w_dq, g_q, w_uq2_t, w_o, k_nope, k_rope, v_t):
    bsz, s, d = h.shape
    qlora = w_dq.shape[1]
    hk = k_nope.shape[2]
    hv = v_t.shape[1]
    tq = Q_TILE
    return pl.pallas_call(
        _mla_kernel,
        grid=(bsz, s // tq),
        in_specs=[_row_spec(tq, d), pl.BlockSpec((None, 1, tq), lambda b, i: (b, 0, i)),
                  _const_spec((QK_ROPE // 2, 1)),
                  _const_spec((1, d)), _const_spec((d, qlora)), _const_spec((1, qlora)),
                  _const_spec((N_HEADS * HEAD_W, qlora)), _const_spec((hv, d)),
                  pl.BlockSpec((None, s, hk), lambda b, i: (b, 0, 0)),
                  pl.BlockSpec((None, s, 2 * QK_ROPE), lambda b, i: (b, 0, 0)),
                  pl.BlockSpec((None, hv, s), lambda b, i: (b, 0, 0))],
        out_specs=_row_spec(tq, d),
        out_shape=jax.ShapeDtypeStruct(h.shape, _F32),
        scratch_shapes=[pltpu.VMEM((N_HEADS * HEAD_W, tq), _BF16),
                        pltpu.VMEM((tq, hv), _F32),
                        pltpu.VMEM((2, s, tq), _F32)],
        compiler_params=_params(),
        name="mla_attention",
    )(h, pos_row, freq_col, g, w_dq, g_q, w_uq2_t, w_o, k_nope, k_rope, v_t)


def _swap_halves(w):
    half = w.shape[-1] // 2
    return jnp.concatenate([w[..., half:], w[..., :half]], axis=-1)


def kernel(x, positions, attn_norm, ffn_norm, final_norm, sc_w_in, sc_conv_w, sc_w_out, kv_in_norm, w_dkv, kv_latent_norm, w_kr, w_uk, w_uv, w_dq, q_latent_norm, w_uq, w_o, ffn_w_up, ffn_conv_w, ffn_conv_b, ffn_w_down):
    depth = attn_norm.shape[0]
    n_a = sc_w_in.shape[0]
    row = lambda v: v.reshape(1, -1)

    half = QK_ROPE // 2
    inv_freq = 1.0 / (ROPE_THETA ** (jnp.arange(half, dtype=_F32) / half))
    freq_col = inv_freq.reshape(half, 1)
    pos_row = positions.reshape(positions.shape[0], 1, positions.shape[1])

    w_kr2 = jnp.concatenate([w_kr, _swap_halves(w_kr)], axis=1)
    ffn_w_up, ffn_w_down = ffn_w_up.astype(_BF16), ffn_w_down.astype(_BF16)
    final_g = row(final_norm)

    h = x
    kv = None
    for layer in range(depth):
        g_attn = row(attn_norm[layer])
        if layer < n_a:
            h = _mixer(h, g_attn, sc_w_in[layer], sc_conv_w[layer], sc_w_out[layer])
        else:
            a = layer - n_a
            wq = w_uq[a].reshape(-1, N_HEADS, QK_NOPE + QK_ROPE)
            rope_cols = wq[:, :, QK_NOPE:]
            w_uq2 = jnp.concatenate([wq, _swap_halves(rope_cols)], axis=2)
            w_uq2_t = w_uq2.reshape(-1, N_HEADS * HEAD_W).T
            h = _mla(h, pos_row, freq_col, g_attn, w_dq[a], row(q_latent_norm[a]), w_uq2_t,
                     w_o[a], *kv)
        h = _ffn(h, row(ffn_norm[layer]), ffn_w_up, ffn_conv_w[layer], row(ffn_conv_b[layer]),
                 ffn_w_down, final_g, layer=layer, final_norm=(layer == depth - 1))
        if layer == n_a - 1:
            kv = _shared_kv(h, pos_row, freq_col, row(kv_in_norm), w_dkv, row(kv_latent_norm),
                            w_kr2, w_uk, w_uv.T)
    return h
```

```python
import functools
import math

import jax
import jax.numpy as jnp
from jax import lax
from jax.experimental import pallas as pl
from jax.experimental.pallas import tpu as pltpu

CHUNK = 64
CONV_W = 3
N_HEADS = 8
QK_NOPE = 128
QK_ROPE = 64
V_HEAD = 128
ROPE_THETA = 10000.0
EPS = 1e-6
NEG_INF = -1e30

V7X_SUBLANES = 8
V7X_LANES = 128
V7X_MXU_DIM = 256
V7X_VMEM_BYTES = 64 * 1024 * 1024

HEAD_W = QK_NOPE + 2 * QK_ROPE
ROW_TILE = 512
Q_TILE = 256
FF_CHUNK = V7X_MXU_DIM
VMEM_LIMIT = V7X_VMEM_BYTES - 8 * 1024 * 1024

_BF16 = jnp.bfloat16
_F32 = jnp.float32


def _dot(a, b):
    return jnp.dot(a, b, preferred_element_type=_F32)


def _rms(x, g):
    return x * lax.rsqrt(jnp.mean(x * x, axis=-1, keepdims=True) + EPS) * g


def _const_spec(shape):
    nd = len(shape)
    return pl.BlockSpec(shape, lambda *_: (0,) * nd, pipeline_mode=pl.Buffered(1))


def _layer_spec(shape, layer):
    nd = len(shape)
    return pl.BlockSpec((None,) + tuple(shape), lambda *_: (layer,) + (0,) * nd,
                        pipeline_mode=pl.Buffered(1))


def _row_spec(tile, width):
    return pl.BlockSpec((None, tile, width), lambda b, i: (b, i, 0))


def _params():
    return pltpu.CompilerParams(
        dimension_semantics=("arbitrary", "arbitrary"), vmem_limit_bytes=VMEM_LIMIT)


def _causal_conv3(buf_ref, cur, w, tile):
    halo = V7X_SUBLANES
    y = buf_ref[pl.ds(halo - 2, tile), :] * w[0:1]
    y = y + buf_ref[pl.ds(halo - 1, tile), :] * w[1:2]
    return y + cur * w[2:3]


def _mixer_kernel(x_ref, g_ref, win_ref, cw_ref, wout_ref, o_ref, cu_ref):
    tile, d = x_ref.shape
    halo = V7X_SUBLANES

    @pl.when(pl.program_id(1) == 0)
    def _():
        cu_ref[0:halo, :] = jnp.zeros((halo, d), _F32)

    x = x_ref[...]
    hn = _rms(x, g_ref[...])
    b_gate = _dot(hn, win_ref[:, 0:d])
    cu = _dot(hn, win_ref[:, d:2 * d]) * _dot(hn, win_ref[:, 2 * d:3 * d])
    cu_ref[halo:halo + tile, :] = cu
    conv = _causal_conv3(cu_ref, cu, cw_ref[...], tile)
    cu_ref[0:halo, :] = cu_ref[tile:tile + halo, :]
    o_ref[...] = x + _dot(b_gate * conv, wout_ref[...])


def _mixer(x, g, w_in, conv_w, w_out):
    bsz, s, d = x.shape
    tile = ROW_TILE
    return pl.pallas_call(
        _mixer_kernel,
        grid=(bsz, s // tile),
        in_specs=[_row_spec(tile, d), _const_spec((1, d)), _const_spec((d, 3 * d)),
                  _const_spec((CONV_W, d)), _const_spec((d, d))],
        out_specs=_row_spec(tile, d),
        out_shape=jax.ShapeDtypeStruct(x.shape, _F32),
        scratch_shapes=[pltpu.VMEM((tile + V7X_SUBLANES, d), _F32)],
        compiler_params=_params(),
        name="sc_mixer",
    )(x, g, w_in, conv_w, w_out)


def _ffn_kernel(h_ref, g_ref, wup_ref, cw_ref, cb_ref, wdn_ref, fg_ref, o_ref,
                gbuf_ref, act_ref, *, final_norm):
    tile, d = h_ref.shape
    dff = act_ref.shape[1]
    halo = V7X_SUBLANES
    ch = FF_CHUNK

    @pl.when(pl.program_id(1) == 0)
    def _():
        gbuf_ref[0:halo, :] = jnp.zeros((halo, dff), _F32)

    h = h_ref[...]
    hn = _rms(h, g_ref[...]).astype(_BF16)
    for c in range(dff // ch):
        cols = slice(c * ch, (c + 1) * ch)
        g = _dot(hn, wup_ref[:, cols])
        v = _dot(hn, wup_ref[:, dff + c * ch:dff + (c + 1) * ch])
        gbuf_ref[halo:halo + tile, cols] = g
        pre = _causal_conv3(gbuf_ref.at[:, cols], g, cw_ref[:, cols], tile) + cb_ref[:, cols]
        gbuf_ref[0:halo, cols] = gbuf_ref[tile:tile + halo, cols]
        act = pre * (1.0 / (1.0 + jnp.exp(-pre))) * v
        act_ref[:, cols] = act.astype(_BF16)
    out = h + _dot(act_ref[...], wdn_ref[...])
    if final_norm:
        out = _rms(out, fg_ref[...])
    o_ref[...] = out


def _ffn(h, g, w_up, conv_w, conv_b, w_down, final_g, *, layer, final_norm):
    bsz, s, d = h.shape
    dff = w_down.shape[1]
    tile = ROW_TILE
    return pl.pallas_call(
        functools.partial(_ffn_kernel, final_norm=final_norm),
        grid=(bsz, s // tile),
        in_specs=[_row_spec(tile, d), _const_spec((1, d)), _layer_spec((d, 2 * dff), layer),
                  _const_spec((CONV_W, dff)), _const_spec((1, dff)), _layer_spec((dff, d), layer),
                  _const_spec((1, d))],
        out_specs=_row_spec(tile, d),
        out_shape=jax.ShapeDtypeStruct(h.shape, _F32),
        scratch_shapes=[pltpu.VMEM((tile + V7X_SUBLANES, dff), _F32),
                        pltpu.VMEM((tile, dff), _BF16)],
        compiler_params=_params(),
        name="conv_ffn_final" if final_norm else "conv_ffn",
    )(h, g, w_up, conv_w, conv_b, w_down, final_g)


def _rope_table_t(pos_row, inv_freq_col, scale):
    ang = pos_row.astype(_F32) * inv_freq_col
    cos, sin = jnp.cos(ang), jnp.sin(ang)
    table = jnp.concatenate([cos, cos, -sin, sin], axis=0)
    return table * scale if scale != 1.0 else table


def _kv_kernel(h_ref, pos_ref, freq_ref, gin_ref, wdkv_ref, glat_ref, wkr_ref, wuk_ref,
               wuvt_ref, kn_ref, kr_ref, vt_ref):
    hn = _rms(h_ref[...], gin_ref[...])
    c_kv = _rms(_dot(hn, wdkv_ref[...]), glat_ref[...])
    p = _dot(hn, wkr_ref[...]) * _rope_table_t(pos_ref[...], freq_ref[...], 1.0).T
    kr_ref[...] = (p + pltpu.roll(p, QK_ROPE, 1)).astype(_BF16)
    kn_ref[...] = _dot(c_kv, wuk_ref[...]).astype(_BF16)
    vt_ref[...] = _dot(wuvt_ref[...], c_kv.T).astype(_BF16)


def _shared_kv(h, pos, freq, g_in, w_dkv, g_lat, w_kr2, w_uk, w_uv_t):
    bsz, s, d = h.shape
    lora = w_dkv.shape[1]
    hk = w_uk.shape[1]
    hv = w_uv_t.shape[0]
    tile = ROW_TILE
    return pl.pallas_call(
        _kv_kernel,
        grid=(bsz, s // tile),
        in_specs=[_row_spec(tile, d), pl.BlockSpec((None, 1, tile), lambda b, i: (b, 0, i)),
                  _const_spec((QK_ROPE // 2, 1)),
                  _const_spec((1, d)), _const_spec((d, lora)), _const_spec((1, lora)),
                  _const_spec((d, 2 * QK_ROPE)), _const_spec((lora, hk)),
                  _const_spec((hv, lora))],
        out_specs=[_row_spec(tile, hk), _row_spec(tile, 2 * QK_ROPE),
                   pl.BlockSpec((None, hv, tile), lambda b, i: (b, 0, i))],
        out_shape=[jax.ShapeDtypeStruct((bsz, s, hk), _BF16),
                   jax.ShapeDtypeStruct((bsz, s, 2 * QK_ROPE), _BF16),
                   jax.ShapeDtypeStruct((bsz, hv, s), _BF16)],
        compiler_params=_params(),
        name="shared_kv",
    )(h, pos, freq, g_in, w_dkv, g_lat, w_kr2, w_uk, w_uv_t)


def _mla_kernel(h_ref, posr_ref, freqc_ref, g_ref, wdq_ref, gq_ref, wuqt_ref, wo_ref,
                kn_ref, kr_ref, vt_ref, o_ref, qt_ref, ob_ref, s_ref):
    tq, d = h_ref.shape
    i = pl.program_id(0)
    scale = float(QK_NOPE + QK_ROPE) ** -0.5 * math.log2(math.e)

    h = h_ref[...]
    hn = _rms(h, g_ref[...])
    c_q = _rms(_dot(hn, wdq_ref[...]), gq_ref[...])
    c_qt = c_q.T
    table = _rope_table_t(posr_ref[...], freqc_ref[...], scale)
    for hd in range(N_HEADS):
        qt = _dot(wuqt_ref[hd * HEAD_W:(hd + 1) * HEAD_W, :], c_qt)
        qt_ref[hd * HEAD_W:hd * HEAD_W + QK_NOPE, :] = (qt[0:QK_NOPE] * scale).astype(_BF16)
        qt_ref[hd * HEAD_W + QK_NOPE:(hd + 1) * HEAD_W, :] = (qt[QK_NOPE:] * table).astype(_BF16)

    ones_rows = 2 * V7X_SUBLANES
    k_chunk = lax.broadcasted_iota(jnp.int32, (tq, tq), 0) // CHUNK
    q_chunk = lax.broadcasted_iota(jnp.int32, (tq, tq), 1) // CHUNK
    diag_mask = k_chunk <= q_chunk

    ones = jnp.ones((ones_rows, tq), _BF16)

    def attend(n_blocks):
        def score_block(hd, t, m8):
            rows = slice(t * tq, (t + 1) * tq)
            k = jnp.concatenate([kn_ref[rows, hd * QK_NOPE:(hd + 1) * QK_NOPE], kr_ref[rows, :]],
                                axis=1)
            s = _dot(k, qt_ref[hd * HEAD_W:(hd + 1) * HEAD_W, :])
            if t == n_blocks - 1:
                s = jnp.where(diag_mask, s, NEG_INF)
            s_ref[hd % 2, rows, :] = s
            blk_max = jnp.max(s.reshape(tq // V7X_SUBLANES, V7X_SUBLANES, tq), axis=0)
            return blk_max if m8 is None else jnp.maximum(m8, blk_max)

        def pv_block(hd, t, m, acc):
            rows = slice(t * tq, (t + 1) * tq)
            p = jnp.exp2(s_ref[hd % 2, rows, :] - m).astype(_BF16)
            v_aug = jnp.concatenate([vt_ref[hd * V_HEAD:(hd + 1) * V_HEAD, rows], ones], axis=0)
            o_aug = _dot(v_aug, p)
            return o_aug if acc is None else acc + o_aug

        m8_next = None
        for t in range(n_blocks):
            m8_next = score_block(0, t, m8_next)
        for hd in range(N_HEADS):
            m = jnp.max(m8_next, axis=0, keepdims=True)
            m8_next, acc = None, None
            for t in range(n_blocks):
                if hd + 1 < N_HEADS:
                    m8_next = score_block(hd + 1, t, m8_next)
                acc = pv_block(hd, t, m, acc)
            o_t = acc[0:V_HEAD] * (1.0 / acc[V_HEAD:V_HEAD + 1])
            ob_ref[:, hd * V_HEAD:(hd + 1) * V_HEAD] = o_t.T

    for c in range(kn_ref.shape[0] // tq):
        pl.when(i == c)(functools.partial(attend, c + 1))

    o_ref[...] = h + _dot(ob_ref[...], wo_ref[...])


def _mla(h, pos_row, freq_col, g, w_dq, g_q, w_uq2_t, w_o, k_nope, k_rope, v_t):
    bsz, s, d = h.shape
    qlora = w_dq.shape[1]
    hk = k_nope.shape[2]
    hv = v_t.shape[1]
    tq = Q_TILE
    return pl.pallas_call(
        _mla_kernel,
        grid=(s // tq, bsz),
        in_specs=[pl.BlockSpec((None, tq, d), lambda i, b: (b, i, 0)),
                  pl.BlockSpec((None, 1, tq), lambda i, b: (b, 0, i)),
                  _const_spec((QK_ROPE // 2, 1)),
                  _const_spec((1, d)), _const_spec((d, qlora)), _const_spec((1, qlora)),
                  _const_spec((N_HEADS * HEAD_W, qlora)), _const_spec((hv, d)),
                  pl.BlockSpec((None, s, hk), lambda i, b: (b, 0, 0)),
                  pl.BlockSpec((None, s, 2 * QK_ROPE), lambda i, b: (b, 0, 0)),
                  pl.BlockSpec((None, hv, s), lambda i, b: (b, 0, 0))],
        out_specs=pl.BlockSpec((None, tq, d), lambda i, b: (b, i, 0)),
        out_shape=jax.ShapeDtypeStruct(h.shape, _F32),
        scratch_shapes=[pltpu.VMEM((N_HEADS * HEAD_W, tq), _BF16),
                        pltpu.VMEM((tq, hv), _F32),
                        pltpu.VMEM((2, s, tq), _F32)],
        compiler_params=_params(),
        name="mla_attention",
    )(h, pos_row, freq_col, g, w_dq, g_q, w_uq2_t, w_o, k_nope, k_rope, v_t)


def _swap_halves(w):
    half = w.shape[-1] // 2
    return jnp.concatenate([w[..., half:], w[..., :half]], axis=-1)


def kernel(x, positions, attn_norm, ffn_norm, final_norm, sc_w_in, sc_conv_w, sc_w_out, kv_in_norm, w_dkv, kv_latent_norm, w_kr, w_uk, w_uv, w_dq, q_latent_norm, w_uq, w_o, ffn_w_up, ffn_conv_w, ffn_conv_b, ffn_w_down):
    depth = attn_norm.shape[0]
    n_a = sc_w_in.shape[0]
    row = lambda v: v.reshape(1, -1)

    half = QK_ROPE // 2
    inv_freq = 1.0 / (ROPE_THETA ** (jnp.arange(half, dtype=_F32) / half))
    freq_col = inv_freq.reshape(half, 1)
    pos_row = positions.reshape(positions.shape[0], 1, positions.shape[1])

    w_kr2 = jnp.concatenate([w_kr, _swap_halves(w_kr)], axis=1)
    ffn_w_up, ffn_w_down = ffn_w_up.astype(_BF16), ffn_w_down.astype(_BF16)
    final_g = row(final_norm)

    h = x
    kv = None
    for layer in range(depth):
        g_attn = row(attn_norm[layer])
        if layer < n_a:
            h = _mixer(h, g_attn, sc_w_in[layer], sc_conv_w[layer], sc_w_out[layer])
        else:
            a = layer - n_a
            wq = w_uq[a].reshape(-1, N_HEADS, QK_NOPE + QK_ROPE)
            rope_cols = wq[:, :, QK_NOPE:]
            w_uq2 = jnp.concatenate([wq, _swap_halves(rope_cols)], axis=2)
            w_uq2_t = w_uq2.reshape(-1, N_HEADS * HEAD_W).T
            h = _mla(h, pos_row, freq_col, g_attn, w_dq[a], row(q_latent_norm[a]), w_uq2_t,
                     w_o[a], *kv)
        h = _ffn(h, row(ffn_norm[layer]), ffn_w_up, ffn_conv_w[layer], row(ffn_conv_b[layer]),
                 ffn_w_down, final_g, layer=layer, final_norm=(layer == depth - 1))
        if layer == n_a - 1:
            kv = _shared_kv(h, pos_row, freq_col, row(kv_in_norm), w_dkv, row(kv_latent_norm),
                            w_kr2, w_uk, w_uv.T)
    return h
```

```python
import functools
import math

import jax
import jax.numpy as jnp
from jax import lax
from jax.experimental import pallas as pl
from jax.experimental.pallas import tpu as pltpu

CHUNK = 64
CONV_W = 3
N_HEADS = 8
QK_NOPE = 128
QK_ROPE = 64
V_HEAD = 128
ROPE_THETA = 10000.0
EPS = 1e-6
NEG_INF = -1e30

V7X_SUBLANES = 8
V7X_LANES = 128
V7X_MXU_DIM = 256
V7X_VMEM_BYTES = 64 * 1024 * 1024

HEAD_W = QK_NOPE + 2 * QK_ROPE
ROW_TILE = 512
Q_TILE = 512
KEY_BLOCK = V7X_MXU_DIM
FF_CHUNK = V7X_MXU_DIM
VMEM_LIMIT = V7X_VMEM_BYTES - 8 * 1024 * 1024

_BF16 = jnp.bfloat16
_F32 = jnp.float32


def _dot(a, b):
    return jnp.dot(a, b, preferred_element_type=_F32)


def _rms(x, g):
    return x * lax.rsqrt(jnp.mean(x * x, axis=-1, keepdims=True) + EPS) * g


def _const_spec(shape):
    nd = len(shape)
    return pl.BlockSpec(shape, lambda *_: (0,) * nd, pipeline_mode=pl.Buffered(1))


def _layer_spec(shape, layer):
    nd = len(shape)
    return pl.BlockSpec((None,) + tuple(shape), lambda *_: (layer,) + (0,) * nd,
                        pipeline_mode=pl.Buffered(1))


def _row_spec(tile, width):
    return pl.BlockSpec((None, tile, width), lambda b, i: (b, i, 0))


def _params():
    return pltpu.CompilerParams(
        dimension_semantics=("arbitrary", "arbitrary"), vmem_limit_bytes=VMEM_LIMIT)


def _causal_conv3(buf_ref, cur, w, tile):
    halo = V7X_SUBLANES
    y = buf_ref[pl.ds(halo - 2, tile), :] * w[0:1]
    y = y + buf_ref[pl.ds(halo - 1, tile), :] * w[1:2]
    return y + cur * w[2:3]


def _mixer_kernel(x_ref, g_ref, win_ref, cw_ref, wout_ref, o_ref, cu_ref):
    tile, d = x_ref.shape
    halo = V7X_SUBLANES

    @pl.when(pl.program_id(1) == 0)
    def _():
        cu_ref[0:halo, :] = jnp.zeros((halo, d), _F32)

    x = x_ref[...]
    hn = _rms(x, g_ref[...])
    b_gate = _dot(hn, win_ref[:, 0:d])
    cu = _dot(hn, win_ref[:, d:2 * d]) * _dot(hn, win_ref[:, 2 * d:3 * d])
    cu_ref[halo:halo + tile, :] = cu
    conv = _causal_conv3(cu_ref, cu, cw_ref[...], tile)
    cu_ref[0:halo, :] = cu_ref[tile:tile + halo, :]
    o_ref[...] = x + _dot(b_gate * conv, wout_ref[...])


def _mixer(x, g, w_in, conv_w, w_out):
    bsz, s, d = x.shape
    tile = ROW_TILE
    return pl.pallas_call(
        _mixer_kernel,
        grid=(bsz, s // tile),
        in_specs=[_row_spec(tile, d), _const_spec((1, d)), _const_spec((d, 3 * d)),
                  _const_spec((CONV_W, d)), _const_spec((d, d))],
        out_specs=_row_spec(tile, d),
        out_shape=jax.ShapeDtypeStruct(x.shape, _F32),
        scratch_shapes=[pltpu.VMEM((tile + V7X_SUBLANES, d), _F32)],
        compiler_params=_params(),
        name="sc_mixer",
    )(x, g, w_in, conv_w, w_out)


def _ffn_kernel(h_ref, g_ref, wup_ref, cw_ref, cb_ref, wdn_ref, fg_ref, o_ref,
                gbuf_ref, act_ref, *, final_norm):
    tile, d = h_ref.shape
    dff = act_ref.shape[1]
    halo = V7X_SUBLANES
    ch = FF_CHUNK

    @pl.when(pl.program_id(1) == 0)
    def _():
        gbuf_ref[0:halo, :] = jnp.zeros((halo, dff), _F32)

    h = h_ref[...]
    hn = _rms(h, g_ref[...]).astype(_BF16)
    for c in range(dff // ch):
        cols = slice(c * ch, (c + 1) * ch)
        g = _dot(hn, wup_ref[:, cols])
        v = _dot(hn, wup_ref[:, dff + c * ch:dff + (c + 1) * ch])
        gbuf_ref[halo:halo + tile, cols] = g
        pre = _causal_conv3(gbuf_ref.at[:, cols], g, cw_ref[:, cols], tile) + cb_ref[:, cols]
        gbuf_ref[0:halo, cols] = gbuf_ref[tile:tile + halo, cols]
        act = pre * (1.0 / (1.0 + jnp.exp(-pre))) * v
        act_ref[:, cols] = act.astype(_BF16)
    out = h + _dot(act_ref[...], wdn_ref[...])
    if final_norm:
        out = _rms(out, fg_ref[...])
    o_ref[...] = out


def _ffn(h, g, w_up, conv_w, conv_b, w_down, final_g, *, layer, final_norm):
    bsz, s, d = h.shape
    dff = w_down.shape[1]
    tile = ROW_TILE
    return pl.pallas_call(
        functools.partial(_ffn_kernel, final_norm=final_norm),
        grid=(bsz, s // tile),
        in_specs=[_row_spec(tile, d), _const_spec((1, d)), _layer_spec((d, 2 * dff), layer),
                  _const_spec((CONV_W, dff)), _const_spec((1, dff)), _layer_spec((dff, d), layer),
                  _const_spec((1, d))],
        out_specs=_row_spec(tile, d),
        out_shape=jax.ShapeDtypeStruct(h.shape, _F32),
        scratch_shapes=[pltpu.VMEM((tile + V7X_SUBLANES, dff), _F32),
                        pltpu.VMEM((tile, dff), _BF16)],
        compiler_params=_params(),
        name="conv_ffn_final" if final_norm else "conv_ffn",
    )(h, g, w_up, conv_w, conv_b, w_down, final_g)


def _rope_table_t(pos_row, inv_freq_col, scale):
    ang = pos_row.astype(_F32) * inv_freq_col
    cos, sin = jnp.cos(ang), jnp.sin(ang)
    table = jnp.concatenate([cos, cos, -sin, sin], axis=0)
    return table * scale if scale != 1.0 else table


def _kv_kernel(h_ref, pos_ref, freq_ref, gin_ref, wdkv_ref, glat_ref, wkr_ref, wuk_ref,
               wuvt_ref, kn_ref, kr_ref, vt_ref):
    hn = _rms(h_ref[...], gin_ref[...])
    c_kv = _rms(_dot(hn, wdkv_ref[...]), glat_ref[...])
    p = _dot(hn, wkr_ref[...]) * _rope_table_t(pos_ref[...], freq_ref[...], 1.0).T
    kr_ref[...] = (p + pltpu.roll(p, QK_ROPE, 1)).astype(_BF16)
    kn_ref[...] = _dot(c_kv, wuk_ref[...]).astype(_BF16)
    vt_ref[...] = _dot(wuvt_ref[...], c_kv.T).astype(_BF16)


def _shared_kv(h, pos, freq, g_in, w_dkv, g_lat, w_kr2, w_uk, w_uv_t):
    bsz, s, d = h.shape
    lora = w_dkv.shape[1]
    hk = w_uk.shape[1]
    hv = w_uv_t.shape[0]
    tile = ROW_TILE
    return pl.pallas_call(
        _kv_kernel,
        grid=(bsz, s // tile),
        in_specs=[_row_spec(tile, d), pl.BlockSpec((None, 1, tile), lambda b, i: (b, 0, i)),
                  _const_spec((QK_ROPE // 2, 1)),
                  _const_spec((1, d)), _const_spec((d, lora)), _const_spec((1, lora)),
                  _const_spec((d, 2 * QK_ROPE)), _const_spec((lora, hk)),
                  _const_spec((hv, lora))],
        out_specs=[_row_spec(tile, hk), _row_spec(tile, 2 * QK_ROPE),
                   pl.BlockSpec((None, hv, tile), lambda b, i: (b, 0, i))],
        out_shape=[jax.ShapeDtypeStruct((bsz, s, hk), _BF16),
                   jax.ShapeDtypeStruct((bsz, s, 2 * QK_ROPE), _BF16),
                   jax.ShapeDtypeStruct((bsz, hv, s), _BF16)],
        compiler_params=_params(),
        name="shared_kv",
    )(h, pos, freq, g_in, w_dkv, g_lat, w_kr2, w_uk, w_uv_t)


def _mla_kernel(h_ref, posr_ref, freqc_ref, g_ref, wdq_ref, gq_ref, wuqt_ref, wo_ref,
                kn_ref, kr_ref, vt_ref, o_ref, qt_ref, ob_ref, s_ref):
    tq, d = h_ref.shape
    i = pl.program_id(1)
    scale = float(QK_NOPE + QK_ROPE) ** -0.5 * math.log2(math.e)

    h = h_ref[...]
    hn = _rms(h, g_ref[...])
    c_q = _rms(_dot(hn, wdq_ref[...]), gq_ref[...])
    c_qt = c_q.T
    table = _rope_table_t(posr_ref[...], freqc_ref[...], scale)
    for hd in range(N_HEADS):
        qt = _dot(wuqt_ref[hd * HEAD_W:(hd + 1) * HEAD_W, :], c_qt)
        qt_ref[hd * HEAD_W:hd * HEAD_W + QK_NOPE, :] = (qt[0:QK_NOPE] * scale).astype(_BF16)
        qt_ref[hd * HEAD_W + QK_NOPE:(hd + 1) * HEAD_W, :] = (qt[QK_NOPE:] * table).astype(_BF16)

    ones_rows = 2 * V7X_SUBLANES
    kb = KEY_BLOCK
    n_sub = tq // kb
    k_chunk = lax.broadcasted_iota(jnp.int32, (kb, kb), 0) // CHUNK
    q_chunk = lax.broadcasted_iota(jnp.int32, (kb, kb), 1) // CHUNK
    diag_mask = k_chunk <= q_chunk
    ones = jnp.ones((ones_rows, kb), _BF16)

    def attend(c):
        n_blocks = (c + 1) * n_sub

        def first_lane(t):
            return max(t - c * n_sub, 0) * kb

        def score_block(hd, t, m8):
            rows = slice(t * kb, (t + 1) * kb)
            lo = first_lane(t)
            k = jnp.concatenate([kn_ref[rows, hd * QK_NOPE:(hd + 1) * QK_NOPE], kr_ref[rows, :]],
                                axis=1)
            s = _dot(k, qt_ref[hd * HEAD_W:(hd + 1) * HEAD_W, lo:tq])
            if t >= c * n_sub:
                s_diag = jnp.where(diag_mask, s[:, 0:kb], NEG_INF)
                s = jnp.concatenate([s_diag, s[:, kb:]], axis=1) if lo + kb < tq else s_diag
            s_ref[hd % 2, rows, lo:tq] = s
            blk_max = jnp.max(s.reshape(kb // V7X_SUBLANES, V7X_SUBLANES, tq - lo), axis=0)
            if m8 is None:
                return blk_max
            if lo == 0:
                return jnp.maximum(m8, blk_max)
            return jnp.concatenate([m8[:, 0:lo], jnp.maximum(m8[:, lo:], blk_max)], axis=1)

        def pv_block(hd, t, m, acc):
            rows = slice(t * kb, (t + 1) * kb)
            lo = first_lane(t)
            p = jnp.exp2(s_ref[hd % 2, rows, lo:tq] - m[:, lo:]).astype(_BF16)
            v_aug = jnp.concatenate([vt_ref[hd * V_HEAD:(hd + 1) * V_HEAD, rows], ones], axis=0)
            o_aug = _dot(v_aug, p)
            if acc is None:
                return o_aug
            if lo == 0:
                return acc + o_aug
            return jnp.concatenate([acc[:, 0:lo], acc[:, lo:] + o_aug], axis=1)

        m8_next = None
        for t in range(n_blocks):
            m8_next = score_block(0, t, m8_next)
        for hd in range(N_HEADS):
            m = jnp.max(m8_next, axis=0, keepdims=True)
            m8_next, acc = None, None
            for t in range(n_blocks):
                if hd + 1 < N_HEADS:
                    m8_next = score_block(hd + 1, t, m8_next)
                acc = pv_block(hd, t, m, acc)
            o_t = acc[0:V_HEAD] * (1.0 / acc[V_HEAD:V_HEAD + 1])
            ob_ref[:, hd * V_HEAD:(hd + 1) * V_HEAD] = o_t.T

    for c in range(kn_ref.shape[0] // tq):
        pl.when(i == c)(functools.partial(attend, c))

    o_ref[...] = h + _dot(ob_ref[...], wo_ref[...])


def _mla(h, pos_row, freq_col, g, w_dq, g_q, w_uq2_t, w_o, k_nope, k_rope, v_t):
    bsz, s, d = h.shape
    qlora = w_dq.shape[1]
    hk = k_nope.shape[2]
    hv = v_t.shape[1]
    tq = Q_TILE
    return pl.pallas_call(
        _mla_kernel,
        grid=(bsz, s // tq),
        in_specs=[_row_spec(tq, d), pl.BlockSpec((None, 1, tq), lambda b, i: (b, 0, i)),
                  _const_spec((QK_ROPE // 2, 1)),
                  _const_spec((1, d)), _const_spec((d, qlora)), _const_spec((1, qlora)),
                  _const_spec((N_HEADS * HEAD_W, qlora)), _const_spec((hv, d)),
                  pl.BlockSpec((None, s, hk), lambda b, i: (b, 0, 0)),
                  pl.BlockSpec((None, s, 2 * QK_ROPE), lambda b, i: (b, 0, 0)),
                  pl.BlockSpec((None, hv, s), lambda b, i: (b, 0, 0))],
        out_specs=_row_spec(tq, d),
        out_shape=jax.ShapeDtypeStruct(h.shape, _F32),
        scratch_shapes=[pltpu.VMEM((N_HEADS * HEAD_W, tq), _BF16),
                        pltpu.VMEM((tq, hv), _F32),
                        pltpu.VMEM((2, s, tq), _F32)],
        compiler_params=_params(),
        name="mla_attention",
    )(h, pos_row, freq_col, g, w_dq, g_q, w_uq2_t, w_o, k_nope, k_rope, v_t)


def _swap_halves(w):
    half = w.shape[-1] // 2
    return jnp.concatenate([w[..., half:], w[..., :half]], axis=-1)


def kernel(x, positions, attn_norm, ffn_norm, final_norm, sc_w_in, sc_conv_w, sc_w_out, kv_in_norm, w_dkv, kv_latent_norm, w_kr, w_uk, w_uv, w_dq, q_latent_norm, w_uq, w_o, ffn_w_up, ffn_conv_w, ffn_conv_b, ffn_w_down):
    depth = attn_norm.shape[0]
    n_a = sc_w_in.shape[0]
    row = lambda v: v.reshape(1, -1)

    half = QK_ROPE // 2
    inv_freq = 1.0 / (ROPE_THETA ** (jnp.arange(half, dtype=_F32) / half))
    freq_col = inv_freq.reshape(half, 1)
    pos_row = positions.reshape(positions.shape[0], 1, positions.shape[1])

    w_kr2 = jnp.concatenate([w_kr, _swap_halves(w_kr)], axis=1)
    ffn_w_up, ffn_w_down = ffn_w_up.astype(_BF16), ffn_w_down.astype(_BF16)
    final_g = row(final_norm)

    h = x
    kv = None
    for layer in range(depth):
        g_attn = row(attn_norm[layer])
        if layer < n_a:
            h = _mixer(h, g_attn, sc_w_in[layer], sc_conv_w[layer], sc_w_out[layer])
        else:
            a = layer - n_a
            wq = w_uq[a].reshape(-1, N_HEADS, QK_NOPE + QK_ROPE)
            rope_cols = wq[:, :, QK_NOPE:]
            w_uq2 = jnp.concatenate([wq, _swap_halves(rope_cols)], axis=2)
            w_uq2_t = w_uq2.reshape(-1, N_HEADS * HEAD_W).T
            h = _mla(h, pos_row, freq_col, g_attn, w_dq[a], row(q_latent_norm[a]), w_uq2_t,
                     w_o[a], *kv)
        h = _ffn(h, row(ffn_norm[layer]), ffn_w_up, ffn_conv_w[layer], row(ffn_conv_b[layer]),
                 ffn_w_down, final_g, layer=layer, final_norm=(layer == depth - 1))
        if layer == n_a - 1:
            kv = _shared_kv(h, pos_row, freq_col, row(kv_in_norm), w_dkv, row(kv_latent_norm),
                            w_kr2, w_uk, w_uv.T)
    return h
```

```python
import functools
import math

import jax
import jax.numpy as jnp
from jax import lax
from jax.experimental import pallas as pl
from jax.experimental.pallas import tpu as pltpu

CHUNK = 64
CONV_W = 3
N_HEADS = 8
QK_NOPE = 128
QK_ROPE = 64
V_HEAD = 128
ROPE_THETA = 10000.0
EPS = 1e-6
NEG_INF = -1e30

V7X_SUBLANES = 8
V7X_LANES = 128
V7X_MXU_DIM = 256
V7X_VMEM_BYTES = 64 * 1024 * 1024

HEAD_W = QK_NOPE + 2 * QK_ROPE
ROW_TILE = 512
FFN_TILE = 1024
Q_TILE = 512
KEY_BLOCK = V7X_MXU_DIM
FF_CHUNK = V7X_MXU_DIM
VMEM_LIMIT = V7X_VMEM_BYTES - 8 * 1024 * 1024

_BF16 = jnp.bfloat16
_F32 = jnp.float32


def _dot(a, b):
    return jnp.dot(a, b, preferred_element_type=_F32)


def _rms(x, g):
    return x * lax.rsqrt(jnp.mean(x * x, axis=-1, keepdims=True) + EPS) * g


def _const_spec(shape):
    nd = len(shape)
    return pl.BlockSpec(shape, lambda *_: (0,) * nd, pipeline_mode=pl.Buffered(1))


def _layer_spec(shape, layer):
    nd = len(shape)
    return pl.BlockSpec((None,) + tuple(shape), lambda *_: (layer,) + (0,) * nd,
                        pipeline_mode=pl.Buffered(1))


def _row_spec(tile, width):
    return pl.BlockSpec((None, tile, width), lambda b, i: (b, i, 0))


def _params():
    return pltpu.CompilerParams(
        dimension_semantics=("arbitrary", "arbitrary"), vmem_limit_bytes=VMEM_LIMIT)


def _causal_conv3(buf_ref, cur, w, tile):
    halo = V7X_SUBLANES
    y = buf_ref[pl.ds(halo - 2, tile), :] * w[0:1]
    y = y + buf_ref[pl.ds(halo - 1, tile), :] * w[1:2]
    return y + cur * w[2:3]


def _mixer_kernel(x_ref, g_ref, win_ref, cw_ref, wout_ref, o_ref, cu_ref):
    tile, d = x_ref.shape
    halo = V7X_SUBLANES

    @pl.when(pl.program_id(1) == 0)
    def _():
        cu_ref[0:halo, :] = jnp.zeros((halo, d), _F32)

    x = x_ref[...]
    hn = _rms(x, g_ref[...])
    b_gate = _dot(hn, win_ref[:, 0:d])
    cu = _dot(hn, win_ref[:, d:2 * d]) * _dot(hn, win_ref[:, 2 * d:3 * d])
    cu_ref[halo:halo + tile, :] = cu
    conv = _causal_conv3(cu_ref, cu, cw_ref[...], tile)
    cu_ref[0:halo, :] = cu_ref[tile:tile + halo, :]
    o_ref[...] = x + _dot(b_gate * conv, wout_ref[...])


def _mixer(x, g, w_in, conv_w, w_out):
    bsz, s, d = x.shape
    tile = ROW_TILE
    return pl.pallas_call(
        _mixer_kernel,
        grid=(bsz, s // tile),
        in_specs=[_row_spec(tile, d), _const_spec((1, d)), _const_spec((d, 3 * d)),
                  _const_spec((CONV_W, d)), _const_spec((d, d))],
        out_specs=_row_spec(tile, d),
        out_shape=jax.ShapeDtypeStruct(x.shape, _F32),
        scratch_shapes=[pltpu.VMEM((tile + V7X_SUBLANES, d), _F32)],
        compiler_params=_params(),
        name="sc_mixer",
    )(x, g, w_in, conv_w, w_out)


def _ffn_kernel(h_ref, g_ref, wup_ref, cw_ref, cb_ref, wdn_ref, fg_ref, o_ref,
                gbuf_ref, halo_ref, act_ref, *, final_norm):
    tile, d = h_ref.shape
    dff = act_ref.shape[1]
    halo = V7X_SUBLANES
    ch = FF_CHUNK

    @pl.when(pl.program_id(1) == 0)
    def _():
        halo_ref[...] = jnp.zeros(halo_ref.shape, _F32)

    h = h_ref[...]
    hn = _rms(h, g_ref[...]).astype(_BF16)
    for c in range(dff // ch):
        cols = slice(c * ch, (c + 1) * ch)
        buf = gbuf_ref.at[:, (c % 2) * ch:(c % 2 + 1) * ch]
        g = _dot(hn, wup_ref[:, cols])
        v = _dot(hn, wup_ref[:, dff + c * ch:dff + (c + 1) * ch])
        buf[0:halo, :] = halo_ref[:, cols]
        buf[halo:halo + tile, :] = g
        pre = _causal_conv3(buf, g, cw_ref[:, cols], tile) + cb_ref[:, cols]
        halo_ref[:, cols] = buf[tile:tile + halo, :]
        act = pre * (1.0 / (1.0 + jnp.exp(-pre))) * v
        act_ref[:, cols] = act.astype(_BF16)
    out = h + _dot(act_ref[...], wdn_ref[...])
    if final_norm:
        out = _rms(out, fg_ref[...])
    o_ref[...] = out


def _ffn(h, g, w_up, conv_w, conv_b, w_down, final_g, *, layer, final_norm):
    bsz, s, d = h.shape
    dff = w_down.shape[1]
    tile = FFN_TILE
    return pl.pallas_call(
        functools.partial(_ffn_kernel, final_norm=final_norm),
        grid=(bsz, s // tile),
        in_specs=[_row_spec(tile, d), _const_spec((1, d)), _layer_spec((d, 2 * dff), layer),
                  _const_spec((CONV_W, dff)), _const_spec((1, dff)), _layer_spec((dff, d), layer),
                  _const_spec((1, d))],
        out_specs=_row_spec(tile, d),
        out_shape=jax.ShapeDtypeStruct(h.shape, _F32),
        scratch_shapes=[pltpu.VMEM((tile + V7X_SUBLANES, 2 * FF_CHUNK), _F32),
                        pltpu.VMEM((V7X_SUBLANES, dff), _F32),
                        pltpu.VMEM((tile, dff), _BF16)],
        compiler_params=_params(),
        name="conv_ffn_final" if final_norm else "conv_ffn",
    )(h, g, w_up, conv_w, conv_b, w_down, final_g)


def _rope_table_t(pos_row, inv_freq_col, scale):
    ang = pos_row.astype(_F32) * inv_freq_col
    cos, sin = jnp.cos(ang), jnp.sin(ang)
    table = jnp.concatenate([cos, cos, -sin, sin], axis=0)
    return table * scale if scale != 1.0 else table


def _kv_kernel(h_ref, pos_ref, freq_ref, gin_ref, wdkv_ref, glat_ref, wkr_ref, wuk_ref,
               wuvt_ref, kn_ref, kr_ref, vt_ref):
    hn = _rms(h_ref[...], gin_ref[...])
    c_kv = _rms(_dot(hn, wdkv_ref[...]), glat_ref[...])
    p = _dot(hn, wkr_ref[...]) * _rope_table_t(pos_ref[...], freq_ref[...], 1.0).T
    kr_ref[...] = (p + pltpu.roll(p, QK_ROPE, 1)).astype(_BF16)
    kn_ref[...] = _dot(c_kv, wuk_ref[...]).astype(_BF16)
    vt_ref[...] = _dot(wuvt_ref[...], c_kv.T).astype(_BF16)


def _shared_kv(h, pos, freq, g_in, w_dkv, g_lat, w_kr2, w_uk, w_uv_t):
    bsz, s, d = h.shape
    lora = w_dkv.shape[1]
    hk = w_uk.shape[1]
    hv = w_uv_t.shape[0]
    tile = FFN_TILE
    return pl.pallas_call(
        _kv_kernel,
        grid=(bsz, s // tile),
        in_specs=[_row_spec(tile, d), pl.BlockSpec((None, 1, tile), lambda b, i: (b, 0, i)),
                  _const_spec((QK_ROPE // 2, 1)),
                  _const_spec((1, d)), _const_spec((d, lora)), _const_spec((1, lora)),
                  _const_spec((d, 2 * QK_ROPE)), _const_spec((lora, hk)),
                  _const_spec((hv, lora))],
        out_specs=[_row_spec(tile, hk), _row_spec(tile, 2 * QK_ROPE),
                   pl.BlockSpec((None, hv, tile), lambda b, i: (b, 0, i))],
        out_shape=[jax.ShapeDtypeStruct((bsz, s, hk), _BF16),
                   jax.ShapeDtypeStruct((bsz, s, 2 * QK_ROPE), _BF16),
                   jax.ShapeDtypeStruct((bsz, hv, s), _BF16)],
        compiler_params=_params(),
        name="shared_kv",
    )(h, pos, freq, g_in, w_dkv, g_lat, w_kr2, w_uk, w_uv_t)


def _mla_kernel(h_ref, posr_ref, freqc_ref, g_ref, wdq_ref, gq_ref, wuqt_ref, wo_ref,
                kn_ref, kr_ref, vt_ref, o_ref, qt_ref, ob_ref, s_ref):
    tq, d = h_ref.shape
    i = pl.program_id(1)
    scale = float(QK_NOPE + QK_ROPE) ** -0.5 * math.log2(math.e)

    h = h_ref[...]
    hn = _rms(h, g_ref[...])
    c_q = _rms(_dot(hn, wdq_ref[...]), gq_ref[...])
    c_qt = c_q.T
    table = _rope_table_t(posr_ref[...], freqc_ref[...], scale)
    for hd in range(N_HEADS):
        qt = _dot(wuqt_ref[hd * HEAD_W:(hd + 1) * HEAD_W, :], c_qt)
        qt_ref[hd * HEAD_W:hd * HEAD_W + QK_NOPE, :] = (qt[0:QK_NOPE] * scale).astype(_BF16)
        qt_ref[hd * HEAD_W + QK_NOPE:(hd + 1) * HEAD_W, :] = (qt[QK_NOPE:] * table).astype(_BF16)

    ones_rows = 2 * V7X_SUBLANES
    kb = KEY_BLOCK
    n_sub = tq // kb
    k_chunk = lax.broadcasted_iota(jnp.int32, (kb, kb), 0) // CHUNK
    q_chunk = lax.broadcasted_iota(jnp.int32, (kb, kb), 1) // CHUNK
    diag_mask = k_chunk <= q_chunk
    ones = jnp.ones((ones_rows, kb), _BF16)

    def attend(c):
        n_blocks = (c + 1) * n_sub

        def first_lane(t):
            return max(t - c * n_sub, 0) * kb

        def score_block(hd, t, m8):
            rows = slice(t * kb, (t + 1) * kb)
            lo = first_lane(t)
            k = jnp.concatenate([kn_ref[rows, hd * QK_NOPE:(hd + 1) * QK_NOPE], kr_ref[rows, :]],
                                axis=1)
            s = _dot(k, qt_ref[hd * HEAD_W:(hd + 1) * HEAD_W, lo:tq])
            if t >= c * n_sub:
                s_diag = jnp.where(diag_mask, s[:, 0:kb], NEG_INF)
                s = jnp.concatenate([s_diag, s[:, kb:]], axis=1) if lo + kb < tq else s_diag
            s_ref[hd % 2, rows, lo:tq] = s
            blk_max = jnp.max(s.reshape(kb // V7X_SUBLANES, V7X_SUBLANES, tq - lo), axis=0)
            if m8 is None:
                return blk_max
            if lo == 0:
                return jnp.maximum(m8, blk_max)
            return jnp.concatenate([m8[:, 0:lo], jnp.maximum(m8[:, lo:], blk_max)], axis=1)

        def pv_block(hd, t, m, acc):
            rows = slice(t * kb, (t + 1) * kb)
            lo = first_lane(t)
            p = jnp.exp2(s_ref[hd % 2, rows, lo:tq] - m[:, lo:]).astype(_BF16)
            v_aug = jnp.concatenate([vt_ref[hd * V_HEAD:(hd + 1) * V_HEAD, rows], ones], axis=0)
            o_aug = _dot(v_aug, p)
            if acc is None:
                return o_aug
            if lo == 0:
                return acc + o_aug
            return jnp.concatenate([acc[:, 0:lo], acc[:, lo:] + o_aug], axis=1)

        m8_next = None
        for t in range(n_blocks):
            m8_next = score_block(0, t, m8_next)
        for hd in range(N_HEADS):
            m = jnp.max(m8_next, axis=0, keepdims=True)
            m8_next, acc = None, None
            for t in range(n_blocks):
                if hd + 1 < N_HEADS:
                    m8_next = score_block(hd + 1, t, m8_next)
                acc = pv_block(hd, t, m, acc)
            o_t = acc[0:V_HEAD] * (1.0 / acc[V_HEAD:V_HEAD + 1])
            ob_ref[:, hd * V_HEAD:(hd + 1) * V_HEAD] = o_t.T

    for c in range(kn_ref.shape[0] // tq):
        pl.when(i == c)(functools.partial(attend, c))

    o_ref[...] = h + _dot(ob_ref[...], wo_ref[...])


def _mla(h, pos_row, freq_col, g, w_dq, g_q, w_uq2_t, w_o, k_nope, k_rope, v_t):
    bsz, s, d = h.shape
    qlora = w_dq.shape[1]
    hk = k_nope.shape[2]
    hv = v_t.shape[1]
    tq = Q_TILE
    return pl.pallas_call(
        _mla_kernel,
        grid=(bsz, s // tq),
        in_specs=[_row_spec(tq, d), pl.BlockSpec((None, 1, tq), lambda b, i: (b, 0, i)),
                  _const_spec((QK_ROPE // 2, 1)),
                  _const_spec((1, d)), _const_spec((d, qlora)), _const_spec((1, qlora)),
                  _const_spec((N_HEADS * HEAD_W, qlora)), _const_spec((hv, d)),
                  pl.BlockSpec((None, s, hk), lambda b, i: (b, 0, 0)),
                  pl.BlockSpec((None, s, 2 * QK_ROPE), lambda b, i: (b, 0, 0)),
                  pl.BlockSpec((None, hv, s), lambda b, i: (b, 0, 0))],
        out_specs=_row_spec(tq, d),
        out_shape=jax.ShapeDtypeStruct(h.shape, _F32),
        scratch_shapes=[pltpu.VMEM((N_HEADS * HEAD_W, tq), _BF16),
                        pltpu.VMEM((tq, hv), _F32),
                        pltpu.VMEM((2, s, tq), _F32)],
        compiler_params=_params(),
        name="mla_attention",
    )(h, pos_row, freq_col, g, w_dq, g_q, w_uq2_t, w_o, k_nope, k_rope, v_t)


def _swap_halves(w):
    half = w.shape[-1] // 2
    return jnp.concatenate([w[..., half:], w[..., :half]], axis=-1)


def kernel(x, positions, attn_norm, ffn_norm, final_norm, sc_w_in, sc_conv_w, sc_w_out, kv_in_norm, w_dkv, kv_latent_norm, w_kr, w_uk, w_uv, w_dq, q_latent_norm, w_uq, w_o, ffn_w_up, ffn_conv_w, ffn_conv_b, ffn_w_down):
    depth = attn_norm.shape[0]
    n_a = sc_w_in.shape[0]
    row = lambda v: v.reshape(1, -1)

    half = QK_ROPE // 2
    inv_freq = 1.0 / (ROPE_THETA ** (jnp.arange(half, dtype=_F32) / half))
    freq_col = inv_freq.reshape(half, 1)
    pos_row = positions.reshape(positions.shape[0], 1, positions.shape[1])

    w_kr2 = jnp.concatenate([w_kr, _swap_halves(w_kr)], axis=1)
    ffn_w_up, ffn_w_down = ffn_w_up.astype(_BF16), ffn_w_down.astype(_BF16)
    final_g = row(final_norm)

    h = x
    kv = None
    for layer in range(depth):
        g_attn = row(attn_norm[layer])
        if layer < n_a:
            h = _mixer(h, g_attn, sc_w_in[layer], sc_conv_w[layer], sc_w_out[layer])
        else:
            a = layer - n_a
            wq = w_uq[a].reshape(-1, N_HEADS, QK_NOPE + QK_ROPE)
            rope_cols = wq[:, :, QK_NOPE:]
            w_uq2 = jnp.concatenate([wq, _swap_halves(rope_cols)], axis=2)
            w_uq2_t = w_uq2.reshape(-1, N_HEADS * HEAD_W).T
            h = _mla(h, pos_row, freq_col, g_attn, w_dq[a], row(q_latent_norm[a]), w_uq2_t,
                     w_o[a], *kv)
        h = _ffn(h, row(ffn_norm[layer]), ffn_w_up, ffn_conv_w[layer], row(ffn_conv_b[layer]),
                 ffn_w_down, final_g, layer=layer, final_norm=(layer == depth - 1))
        if layer == n_a - 1:
            kv = _shared_kv(h, pos_row, freq_col, row(kv_in_norm), w_dkv, row(kv_latent_norm),
                            w_kr2, w_uk, w_uv.T)
    return h
```

```python
import functools
import math

import jax
import jax.numpy as jnp
from jax import lax
from jax.experimental import pallas as pl
from jax.experimental.pallas import tpu as pltpu

CHUNK = 64
CONV_W = 3
N_HEADS = 8
QK_NOPE = 128
QK_ROPE = 64
V_HEAD = 128
ROPE_THETA = 10000.0
EPS = 1e-6
NEG_INF = -1e30

V7X_SUBLANES = 8
V7X_LANES = 128
V7X_MXU_DIM = 256
V7X_VMEM_BYTES = 64 * 1024 * 1024

HEAD_W = QK_NOPE + 2 * QK_ROPE
ROW_TILE = 512
KV_TILE = 1024
Q_TILE = 512
KEY_BLOCK = V7X_MXU_DIM
FF_CHUNK = V7X_MXU_DIM
VMEM_LIMIT = V7X_VMEM_BYTES - 8 * 1024 * 1024

_BF16 = jnp.bfloat16
_F32 = jnp.float32


def _dot(a, b):
    return jnp.dot(a, b, preferred_element_type=_F32)


def _rms(x, g):
    return x * lax.rsqrt(jnp.mean(x * x, axis=-1, keepdims=True) + EPS) * g


def _const_spec(shape):
    nd = len(shape)
    return pl.BlockSpec(shape, lambda *_: (0,) * nd, pipeline_mode=pl.Buffered(1))


def _row_spec(tile, width):
    return pl.BlockSpec((None, tile, width), lambda b, i: (b, i, 0))


def _params():
    return pltpu.CompilerParams(
        dimension_semantics=("arbitrary", "arbitrary"), vmem_limit_bytes=VMEM_LIMIT)


def _causal_conv3(buf_ref, cur, w, tile):
    halo = V7X_SUBLANES
    y = buf_ref[pl.ds(halo - 2, tile), :] * w[0:1]
    y = y + buf_ref[pl.ds(halo - 1, tile), :] * w[1:2]
    return y + cur * w[2:3]


def _mixer_kernel(x_ref, g_ref, win_ref, cw_ref, wout_ref, o_ref, cu_ref):
    tile, d = x_ref.shape
    halo = V7X_SUBLANES

    @pl.when(pl.program_id(1) == 0)
    def _():
        cu_ref[0:halo, :] = jnp.zeros((halo, d), _F32)

    x = x_ref[...]
    hn = _rms(x, g_ref[...])
    b_gate = _dot(hn, win_ref[:, 0:d])
    cu = _dot(hn, win_ref[:, d:2 * d]) * _dot(hn, win_ref[:, 2 * d:3 * d])
    cu_ref[halo:halo + tile, :] = cu
    conv = _causal_conv3(cu_ref, cu, cw_ref[...], tile)
    cu_ref[0:halo, :] = cu_ref[tile:tile + halo, :]
    o_ref[...] = x + _dot(b_gate * conv, wout_ref[...])


def _mixer(x, g, w_in, conv_w, w_out):
    bsz, s, d = x.shape
    tile = ROW_TILE
    return pl.pallas_call(
        _mixer_kernel,
        grid=(bsz, s // tile),
        in_specs=[_row_spec(tile, d), _const_spec((1, d)), _const_spec((d, 3 * d)),
                  _const_spec((CONV_W, d)), _const_spec((d, d))],
        out_specs=_row_spec(tile, d),
        out_shape=jax.ShapeDtypeStruct(x.shape, _F32),
        scratch_shapes=[pltpu.VMEM((tile + V7X_SUBLANES, d), _F32)],
        compiler_params=_params(),
        name="sc_mixer",
    )(x, g, w_in, conv_w, w_out)


def _ffn_kernel(h_ref, g_ref, wup_hbm, cw_ref, cb_ref, wdn_hbm, fg_ref, o_ref,
                wup_ref, wdn_ref, stage_up, stage_dn, sems, gbuf_ref, act_ref, *, layer, final_norm):
    tile, d = h_ref.shape
    dff = act_ref.shape[1]
    halo = V7X_SUBLANES
    ch = FF_CHUNK
    n_chunks = dff // ch

    def weight_copies(c):
        slot = c % 2
        return (
            pltpu.make_async_copy(wup_hbm.at[layer, :, pl.ds(c * ch, ch)],
                                  stage_up.at[slot, 0], sems.at[slot, 0]),
            pltpu.make_async_copy(wup_hbm.at[layer, :, pl.ds(dff + c * ch, ch)],
                                  stage_up.at[slot, 1], sems.at[slot, 1]),
            pltpu.make_async_copy(wdn_hbm.at[layer, pl.ds(c * ch, ch), :],
                                  stage_dn.at[slot], sems.at[slot, 2]),
        )

    @pl.when(jnp.logical_and(pl.program_id(0) == 0, pl.program_id(1) == 0))
    def _():
        for cp in weight_copies(0):
            cp.start()
        for c in range(n_chunks):
            if c + 1 < n_chunks:
                for cp in weight_copies(c + 1):
                    cp.start()
            for cp in weight_copies(c):
                cp.wait()
            wup_ref[:, c * ch:(c + 1) * ch] = stage_up[c % 2, 0].astype(_BF16)
            wup_ref[:, dff + c * ch:dff + (c + 1) * ch] = stage_up[c % 2, 1].astype(_BF16)
            wdn_ref[c * ch:(c + 1) * ch, :] = stage_dn[c % 2].astype(_BF16)

    @pl.when(pl.program_id(1) == 0)
    def _():
        gbuf_ref[0:halo, :] = jnp.zeros((halo, dff), _F32)

    h = h_ref[...]
    hn = _rms(h, g_ref[...]).astype(_BF16)
    for c in range(n_chunks):
        cols = slice(c * ch, (c + 1) * ch)
        g = _dot(hn, wup_ref[:, cols])
        v = _dot(hn, wup_ref[:, dff + c * ch:dff + (c + 1) * ch])
        gbuf_ref[halo:halo + tile, cols] = g
        pre = _causal_conv3(gbuf_ref.at[:, cols], g, cw_ref[:, cols], tile) + cb_ref[:, cols]
        gbuf_ref[0:halo, cols] = gbuf_ref[tile:tile + halo, cols]
        act = pre * (1.0 / (1.0 + jnp.exp(-pre))) * v
        act_ref[:, cols] = act.astype(_BF16)
    out = h + _dot(act_ref[...], wdn_ref[...])
    if final_norm:
        out = _rms(out, fg_ref[...])
    o_ref[...] = out


def _ffn(h, g, w_up, conv_w, conv_b, w_down, final_g, *, layer, final_norm):
    bsz, s, d = h.shape
    dff = w_down.shape[1]
    tile = ROW_TILE
    hbm = pl.BlockSpec(memory_space=pl.ANY)
    return pl.pallas_call(
        functools.partial(_ffn_kernel, layer=layer, final_norm=final_norm),
        grid=(bsz, s // tile),
        in_specs=[_row_spec(tile, d), _const_spec((1, d)), hbm,
                  _const_spec((CONV_W, dff)), _const_spec((1, dff)), hbm,
                  _const_spec((1, d))],
        out_specs=_row_spec(tile, d),
        out_shape=jax.ShapeDtypeStruct(h.shape, _F32),
        scratch_shapes=[pltpu.VMEM((d, 2 * dff), _BF16),
                        pltpu.VMEM((dff, d), _BF16),
                        pltpu.VMEM((2, 2, d, FF_CHUNK), _F32),
                        pltpu.VMEM((2, FF_CHUNK, d), _F32),
                        pltpu.SemaphoreType.DMA((2, 3)),
                        pltpu.VMEM((tile + V7X_SUBLANES, dff), _F32),
                        pltpu.VMEM((tile, dff), _BF16)],
        compiler_params=_params(),
        name="conv_ffn_final" if final_norm else "conv_ffn",
    )(h, g, w_up, conv_w, conv_b, w_down, final_g)


def _rope_table_t(pos_row, inv_freq_col, scale):
    ang = pos_row.astype(_F32) * inv_freq_col
    cos, sin = jnp.cos(ang), jnp.sin(ang)
    table = jnp.concatenate([cos, cos, -sin, sin], axis=0)
    return table * scale if scale != 1.0 else table


def _kv_kernel(h_ref, pos_ref, freq_ref, gin_ref, wdkv_ref, glat_ref, wkr_ref, wuk_ref,
               wuvt_ref, kn_ref, kr_ref, vt_ref):
    hn = _rms(h_ref[...], gin_ref[...])
    c_kv = _rms(_dot(hn, wdkv_ref[...]), glat_ref[...])
    p = _dot(hn, wkr_ref[...]) * _rope_table_t(pos_ref[...], freq_ref[...], 1.0).T
    kr_ref[...] = (p + pltpu.roll(p, QK_ROPE, 1)).astype(_BF16)
    kn_ref[...] = _dot(c_kv, wuk_ref[...]).astype(_BF16)
    vt_ref[...] = _dot(wuvt_ref[...], c_kv.T).astype(_BF16)


def _shared_kv(h, pos, freq, g_in, w_dkv, g_lat, w_kr2, w_uk, w_uv_t):
    bsz, s, d = h.shape
    lora = w_dkv.shape[1]
    hk = w_uk.shape[1]
    hv = w_uv_t.shape[0]
    tile = KV_TILE
    return pl.pallas_call(
        _kv_kernel,
        grid=(bsz, s // tile),
        in_specs=[_row_spec(tile, d), pl.BlockSpec((None, 1, tile), lambda b, i: (b, 0, i)),
                  _const_spec((QK_ROPE // 2, 1)),
                  _const_spec((1, d)), _const_spec((d, lora)), _const_spec((1, lora)),
                  _const_spec((d, 2 * QK_ROPE)), _const_spec((lora, hk)),
                  _const_spec((hv, lora))],
        out_specs=[_row_spec(tile, hk), _row_spec(tile, 2 * QK_ROPE),
                   pl.BlockSpec((None, hv, tile), lambda b, i: (b, 0, i))],
        out_shape=[jax.ShapeDtypeStruct((bsz, s, hk), _BF16),
                   jax.ShapeDtypeStruct((bsz, s, 2 * QK_ROPE), _BF16),
                   jax.ShapeDtypeStruct((bsz, hv, s), _BF16)],
        compiler_params=_params(),
        name="shared_kv",
    )(h, pos, freq, g_in, w_dkv, g_lat, w_kr2, w_uk, w_uv_t)


def _mla_kernel(h_ref, posr_ref, freqc_ref, g_ref, wdq_ref, gq_ref, wuqt_ref, wo_ref,
                kn_ref, kr_ref, vt_ref, o_ref, qt_ref, ob_ref, s_ref):
    tq, d = h_ref.shape
    i = pl.program_id(1)
    scale = float(QK_NOPE + QK_ROPE) ** -0.5 * math.log2(math.e)

    h = h_ref[...]
    hn = _rms(h, g_ref[...])
    c_q = _rms(_dot(hn, wdq_ref[...]), gq_ref[...])
    c_qt = c_q.T
    table = _rope_table_t(posr_ref[...], freqc_ref[...], scale)
    for hd in range(N_HEADS):
        qt = _dot(wuqt_ref[hd * HEAD_W:(hd + 1) * HEAD_W, :], c_qt)
        qt_ref[hd * HEAD_W:hd * HEAD_W + QK_NOPE, :] = (qt[0:QK_NOPE] * scale).astype(_BF16)
        qt_ref[hd * HEAD_W + QK_NOPE:(hd + 1) * HEAD_W, :] = (qt[QK_NOPE:] * table).astype(_BF16)

    ones_rows = 2 * V7X_SUBLANES
    kb = KEY_BLOCK
    n_sub = tq // kb
    k_chunk = lax.broadcasted_iota(jnp.int32, (kb, kb), 0) // CHUNK
    q_chunk = lax.broadcasted_iota(jnp.int32, (kb, kb), 1) // CHUNK
    diag_mask = k_chunk <= q_chunk
    ones = jnp.ones((ones_rows, kb), _BF16)

    def attend(c):
        n_blocks = (c + 1) * n_sub

        def first_lane(t):
            return max(t - c * n_sub, 0) * kb

        def score_block(hd, t, m8):
            rows = slice(t * kb, (t + 1) * kb)
            lo = first_lane(t)
            k = jnp.concatenate([kn_ref[rows, hd * QK_NOPE:(hd + 1) * QK_NOPE], kr_ref[rows, :]],
                                axis=1)
            s = _dot(k, qt_ref[hd * HEAD_W:(hd + 1) * HEAD_W, lo:tq])
            if t >= c * n_sub:
                s_diag = jnp.where(diag_mask, s[:, 0:kb], NEG_INF)
                s = jnp.concatenate([s_diag, s[:, kb:]], axis=1) if lo + kb < tq else s_diag
            s_ref[hd % 2, rows, lo:tq] = s
            blk_max = jnp.max(s.reshape(kb // V7X_SUBLANES, V7X_SUBLANES, tq - lo), axis=0)
            if m8 is None:
                return blk_max
            if lo == 0:
                return jnp.maximum(m8, blk_max)
            return jnp.concatenate([m8[:, 0:lo], jnp.maximum(m8[:, lo:], blk_max)], axis=1)

        def pv_block(hd, t, m, acc):
            rows = slice(t * kb, (t + 1) * kb)
            lo = first_lane(t)
            p = jnp.exp2(s_ref[hd % 2, rows, lo:tq] - m[:, lo:]).astype(_BF16)
            v_aug = jnp.concatenate([vt_ref[hd * V_HEAD:(hd + 1) * V_HEAD, rows], ones], axis=0)
            o_aug = _dot(v_aug, p)
            if acc is None:
                return o_aug
            if lo == 0:
                return acc + o_aug
            return jnp.concatenate([acc[:, 0:lo], acc[:, lo:] + o_aug], axis=1)

        m8_next = None
        for t in range(n_blocks):
            m8_next = score_block(0, t, m8_next)
        for hd in range(N_HEADS):
            m = jnp.max(m8_next, axis=0, keepdims=True)
            m8_next, acc = None, None
            for t in range(n_blocks):
                if hd + 1 < N_HEADS:
                    m8_next = score_block(hd + 1, t, m8_next)
                acc = pv_block(hd, t, m, acc)
            o_t = acc[0:V_HEAD] * (1.0 / acc[V_HEAD:V_HEAD + 1])
            ob_ref[:, hd * V_HEAD:(hd + 1) * V_HEAD] = o_t.T

    for c in range(kn_ref.shape[0] // tq):
        pl.when(i == c)(functools.partial(attend, c))

    o_ref[...] = h + _dot(ob_ref[...], wo_ref[...])


def _mla(h, pos_row, freq_col, g, w_dq, g_q, w_uq2_t, w_o, k_nope, k_rope, v_t):
    bsz, s, d = h.shape
    qlora = w_dq.shape[1]
    hk = k_nope.shape[2]
    hv = v_t.shape[1]
    tq = Q_TILE
    return pl.pallas_call(
        _mla_kernel,
        grid=(bsz, s // tq),
        in_specs=[_row_spec(tq, d), pl.BlockSpec((None, 1, tq), lambda b, i: (b, 0, i)),
                  _const_spec((QK_ROPE // 2, 1)),
                  _const_spec((1, d)), _const_spec((d, qlora)), _const_spec((1, qlora)),
                  _const_spec((N_HEADS * HEAD_W, qlora)), _const_spec((hv, d)),
                  pl.BlockSpec((None, s, hk), lambda b, i: (b, 0, 0)),
                  pl.BlockSpec((None, s, 2 * QK_ROPE), lambda b, i: (b, 0, 0)),
                  pl.BlockSpec((None, hv, s), lambda b, i: (b, 0, 0))],
        out_specs=_row_spec(tq, d),
        out_shape=jax.ShapeDtypeStruct(h.shape, _F32),
        scratch_shapes=[pltpu.VMEM((N_HEADS * HEAD_W, tq), _BF16),
                        pltpu.VMEM((tq, hv), _F32),
                        pltpu.VMEM((2, s, tq), _F32)],
        compiler_params=_params(),
        name="mla_attention",
    )(h, pos_row, freq_col, g, w_dq, g_q, w_uq2_t, w_o, k_nope, k_rope, v_t)


def _swap_halves(w):
    half = w.shape[-1] // 2
    return jnp.concatenate([w[..., half:], w[..., :half]], axis=-1)


def kernel(x, positions, attn_norm, ffn_norm, final_norm, sc_w_in, sc_conv_w, sc_w_out, kv_in_norm, w_dkv, kv_latent_norm, w_kr, w_uk, w_uv, w_dq, q_latent_norm, w_uq, w_o, ffn_w_up, ffn_conv_w, ffn_conv_b, ffn_w_down):
    depth = attn_norm.shape[0]
    n_a = sc_w_in.shape[0]
    row = lambda v: v.reshape(1, -1)

    half = QK_ROPE // 2
    inv_freq = 1.0 / (ROPE_THETA ** (jnp.arange(half, dtype=_F32) / half))
    freq_col = inv_freq.reshape(half, 1)
    pos_row = positions.reshape(positions.shape[0], 1, positions.shape[1])

    w_kr2 = jnp.concatenate([w_kr, _swap_halves(w_kr)], axis=1)
    final_g = row(final_norm)

    h = x
    kv = None
    for layer in range(depth):
        g_attn = row(attn_norm[layer])
        if layer < n_a:
            h = _mixer(h, g_attn, sc_w_in[layer], sc_conv_w[layer], sc_w_out[layer])
        else:
            a = layer - n_a
            wq = w_uq[a].reshape(-1, N_HEADS, QK_NOPE + QK_ROPE)
            rope_cols = wq[:, :, QK_NOPE:]
            w_uq2 = jnp.concatenate([wq, _swap_halves(rope_cols)], axis=2)
            w_uq2_t = w_uq2.reshape(-1, N_HEADS * HEAD_W).T
            h = _mla(h, pos_row, freq_col, g_attn, w_dq[a], row(q_latent_norm[a]), w_uq2_t,
                     w_o[a], *kv)
        h = _ffn(h, row(ffn_norm[layer]), ffn_w_up, ffn_conv_w[layer], row(ffn_conv_b[layer]),
                 ffn_w_down, final_g, layer=layer, final_norm=(layer == depth - 1))
        if layer == n_a - 1:
            kv = _shared_kv(h, pos_row, freq_col, row(kv_in_norm), w_dkv, row(kv_latent_norm),
                            w_kr2, w_uk, w_uv.T)
    return h
```

```python
import functools
import math

import jax
import jax.numpy as jnp
from jax import lax
from jax.experimental import pallas as pl
from jax.experimental.pallas import tpu as pltpu

CHUNK = 64
CONV_W = 3
N_HEADS = 8
QK_NOPE = 128
QK_ROPE = 64
V_HEAD = 128
ROPE_THETA = 10000.0
EPS = 1e-6
NEG_INF = -1e30

V7X_SUBLANES = 8
V7X_LANES = 128
V7X_MXU_DIM = 256
V7X_VMEM_BYTES = 64 * 1024 * 1024

HEAD_W = QK_NOPE + 2 * QK_ROPE
ROW_TILE = 512
KV_TILE = 1024
Q_TILE = 512
KEY_BLOCK = V7X_MXU_DIM
FF_CHUNK = V7X_MXU_DIM
VMEM_LIMIT = V7X_VMEM_BYTES - 8 * 1024 * 1024

_BF16 = jnp.bfloat16
_F32 = jnp.float32


def _dot(a, b):
    return jnp.dot(a, b, preferred_element_type=_F32)


def _rms(x, g):
    return x * lax.rsqrt(jnp.mean(x * x, axis=-1, keepdims=True) + EPS) * g


def _const_spec(shape):
    nd = len(shape)
    return pl.BlockSpec(shape, lambda *_: (0,) * nd, pipeline_mode=pl.Buffered(1))


def _row_spec(tile, width):
    return pl.BlockSpec((None, tile, width), lambda b, i: (b, i, 0))


def _params():
    return pltpu.CompilerParams(
        dimension_semantics=("arbitrary", "arbitrary"), vmem_limit_bytes=VMEM_LIMIT)


def _causal_conv3(slab_ref, first, cur, w):
    halo, lanes = V7X_SUBLANES, V7X_LANES
    tile = cur.shape[0]
    outs = []
    for j in range(cur.shape[1] // lanes):
        cols = slice(j * lanes, (j + 1) * lanes)
        slab = slab_ref.at[first + j]
        slab[halo:halo + tile, :] = cur[:, cols]
        y = slab[pl.ds(halo - 2, tile), :] * w[0:1, cols]
        y = y + slab[pl.ds(halo - 1, tile), :] * w[1:2, cols]
        outs.append(y + cur[:, cols] * w[2:3, cols])
        slab[0:halo, :] = slab[tile:tile + halo, :]
    return jnp.concatenate(outs, axis=1)


def _mixer_kernel(x_ref, g_ref, win_ref, cw_ref, wout_ref, o_ref, cu_ref):
    tile, d = x_ref.shape
    halo = V7X_SUBLANES


    @pl.when(pl.program_id(1) == 0)
    def _():
        cu_ref[:, 0:halo, :] = jnp.zeros((cu_ref.shape[0], halo, cu_ref.shape[2]), _F32)

    x = x_ref[...]
    hn = _rms(x, g_ref[...])
    b_gate = _dot(hn, win_ref[:, 0:d])
    cu = _dot(hn, win_ref[:, d:2 * d]) * _dot(hn, win_ref[:, 2 * d:3 * d])
    conv = _causal_conv3(cu_ref, 0, cu, cw_ref[...])
    o_ref[...] = x + _dot(b_gate * conv, wout_ref[...])


def _mixer(x, g, w_in, conv_w, w_out):
    bsz, s, d = x.shape
    tile = ROW_TILE
    return pl.pallas_call(
        _mixer_kernel,
        grid=(bsz, s // tile),
        in_specs=[_row_spec(tile, d), _const_spec((1, d)), _const_spec((d, 3 * d)),
                  _const_spec((CONV_W, d)), _const_spec((d, d))],
        out_specs=_row_spec(tile, d),
        out_shape=jax.ShapeDtypeStruct(x.shape, _F32),
        scratch_shapes=[pltpu.VMEM((d // V7X_LANES, tile + V7X_SUBLANES, V7X_LANES), _F32)],
        compiler_params=_params(),
        name="sc_mixer",
    )(x, g, w_in, conv_w, w_out)


def _ffn_kernel(h_ref, g_ref, wup_hbm, cw_ref, cb_ref, wdn_hbm, fg_ref, o_ref,
                wup_ref, wdn_ref, stage_up, stage_dn, sems, gbuf_ref, act_ref, *, layer, final_norm):
    tile, d = h_ref.shape
    dff = act_ref.shape[1]
    halo = V7X_SUBLANES
    ch = FF_CHUNK
    n_chunks = dff // ch

    def weight_copies(c):
        slot = c % 2
        return (
            pltpu.make_async_copy(wup_hbm.at[layer, :, pl.ds(c * ch, ch)],
                                  stage_up.at[slot, 0], sems.at[slot, 0]),
            pltpu.make_async_copy(wup_hbm.at[layer, :, pl.ds(dff + c * ch, ch)],
                                  stage_up.at[slot, 1], sems.at[slot, 1]),
            pltpu.make_async_copy(wdn_hbm.at[layer, pl.ds(c * ch, ch), :],
                                  stage_dn.at[slot], sems.at[slot, 2]),
        )

    @pl.when(jnp.logical_and(pl.program_id(0) == 0, pl.program_id(1) == 0))
    def _():
        for cp in weight_copies(0):
            cp.start()
        for c in range(n_chunks):
            if c + 1 < n_chunks:
                for cp in weight_copies(c + 1):
                    cp.start()
            for cp in weight_copies(c):
                cp.wait()
            wup_ref[:, c * ch:(c + 1) * ch] = stage_up[c % 2, 0].astype(_BF16)
            wup_ref[:, dff + c * ch:dff + (c + 1) * ch] = stage_up[c % 2, 1].astype(_BF16)
            wdn_ref[c * ch:(c + 1) * ch, :] = stage_dn[c % 2].astype(_BF16)

    @pl.when(pl.program_id(1) == 0)
    def _():
        gbuf_ref[:, 0:halo, :] = jnp.zeros((gbuf_ref.shape[0], halo, gbuf_ref.shape[2]), _F32)

    h = h_ref[...]
    hn = _rms(h, g_ref[...]).astype(_BF16)
    for c in range(n_chunks):
        cols = slice(c * ch, (c + 1) * ch)
        g = _dot(hn, wup_ref[:, cols])
        v = _dot(hn, wup_ref[:, dff + c * ch:dff + (c + 1) * ch])
        pre = _causal_conv3(gbuf_ref, c * (ch // V7X_LANES), g, cw_ref[:, cols]) + cb_ref[:, cols]
        act = pre * (1.0 / (1.0 + jnp.exp(-pre))) * v
        act_ref[:, cols] = act.astype(_BF16)
    out = h + _dot(act_ref[...], wdn_ref[...])
    if final_norm:
        out = _rms(out, fg_ref[...])
    o_ref[...] = out


def _ffn(h, g, w_up, conv_w, conv_b, w_down, final_g, *, layer, final_norm):
    bsz, s, d = h.shape
    dff = w_down.shape[1]
    tile = ROW_TILE
    hbm = pl.BlockSpec(memory_space=pl.ANY)
    return pl.pallas_call(
        functools.partial(_ffn_kernel, layer=layer, final_norm=final_norm),
        grid=(bsz, s // tile),
        in_specs=[_row_spec(tile, d), _const_spec((1, d)), hbm,
                  _const_spec((CONV_W, dff)), _const_spec((1, dff)), hbm,
                  _const_spec((1, d))],
        out_specs=_row_spec(tile, d),
        out_shape=jax.ShapeDtypeStruct(h.shape, _F32),
        scratch_shapes=[pltpu.VMEM((d, 2 * dff), _BF16),
                        pltpu.VMEM((dff, d), _BF16),
                        pltpu.VMEM((2, 2, d, FF_CHUNK), _F32),
                        pltpu.VMEM((2, FF_CHUNK, d), _F32),
                        pltpu.SemaphoreType.DMA((2, 3)),
                        pltpu.VMEM((dff // V7X_LANES, tile + V7X_SUBLANES, V7X_LANES), _F32),
                        pltpu.VMEM((tile, dff), _BF16)],
        compiler_params=_params(),
        name="conv_ffn_final" if final_norm else "conv_ffn",
    )(h, g, w_up, conv_w, conv_b, w_down, final_g)


def _rope_table_t(pos_row, inv_freq_col, scale):
    ang = pos_row.astype(_F32) * inv_freq_col
    cos, sin = jnp.cos(ang), jnp.sin(ang)
    table = jnp.concatenate([cos, cos, -sin, sin], axis=0)
    return table * scale if scale != 1.0 else table


def _kv_kernel(h_ref, pos_ref, freq_ref, gin_ref, wdkv_ref, glat_ref, wkr_ref, wuk_ref,
               wuvt_ref, kn_ref, kr_ref, vt_ref):
    hn = _rms(h_ref[...], gin_ref[...])
    c_kv = _rms(_dot(hn, wdkv_ref[...]), glat_ref[...])
    p = _dot(hn, wkr_ref[...]) * _rope_table_t(pos_ref[...], freq_ref[...], 1.0).T
    kr_ref[...] = (p + pltpu.roll(p, QK_ROPE, 1)).astype(_BF16)
    kn_ref[...] = _dot(c_kv, wuk_ref[...]).astype(_BF16)
    vt_ref[...] = _dot(wuvt_ref[...], c_kv.T).astype(_BF16)


def _shared_kv(h, pos, freq, g_in, w_dkv, g_lat, w_kr2, w_uk, w_uv_t):
    bsz, s, d = h.shape
    lora = w_dkv.shape[1]
    hk = w_uk.shape[1]
    hv = w_uv_t.shape[0]
    tile = KV_TILE
    return pl.pallas_call(
        _kv_kernel,
        grid=(bsz, s // tile),
        in_specs=[_row_spec(tile, d), pl.BlockSpec((None, 1, tile), lambda b, i: (b, 0, i)),
                  _const_spec((QK_ROPE // 2, 1)),
                  _const_spec((1, d)), _const_spec((d, lora)), _const_spec((1, lora)),
                  _const_spec((d, 2 * QK_ROPE)), _const_spec((lora, hk)),
                  _const_spec((hv, lora))],
        out_specs=[_row_spec(tile, hk), _row_spec(tile, 2 * QK_ROPE),
                   pl.BlockSpec((None, hv, tile), lambda b, i: (b, 0, i))],
        out_shape=[jax.ShapeDtypeStruct((bsz, s, hk), _BF16),
                   jax.ShapeDtypeStruct((bsz, s, 2 * QK_ROPE), _BF16),
                   jax.ShapeDtypeStruct((bsz, hv, s), _BF16)],
        compiler_params=_params(),
        name="shared_kv",
    )(h, pos, freq, g_in, w_dkv, g_lat, w_kr2, w_uk, w_uv_t)


def _mla_kernel(h_ref, posr_ref, freqc_ref, g_ref, wdq_ref, gq_ref, wuqt_ref, wo_ref,
                kn_ref, kr_ref, vt_ref, o_ref, qt_ref, ob_ref, s_ref):
    tq, d = h_ref.shape
    i = pl.program_id(1)
    scale = float(QK_NOPE + QK_ROPE) ** -0.5 * math.log2(math.e)

    h = h_ref[...]
    hn = _rms(h, g_ref[...])
    c_q = _rms(_dot(hn, wdq_ref[...]), gq_ref[...])
    c_qt = c_q.T
    table = _rope_table_t(posr_ref[...], freqc_ref[...], scale)
    for hd in range(N_HEADS):
        qt = _dot(wuqt_ref[hd * HEAD_W:(hd + 1) * HEAD_W, :], c_qt)
        qt_ref[hd * HEAD_W:hd * HEAD_W + QK_NOPE, :] = (qt[0:QK_NOPE] * scale).astype(_BF16)
        qt_ref[hd * HEAD_W + QK_NOPE:(hd + 1) * HEAD_W, :] = (qt[QK_NOPE:] * table).astype(_BF16)

    ones_rows = 2 * V7X_SUBLANES
    kb = KEY_BLOCK
    n_sub = tq // kb
    k_chunk = lax.broadcasted_iota(jnp.int32, (kb, kb), 0) // CHUNK
    q_chunk = lax.broadcasted_iota(jnp.int32, (kb, kb), 1) // CHUNK
    diag_mask = k_chunk <= q_chunk
    ones = jnp.ones((ones_rows, kb), _BF16)

    def attend(c):
        n_blocks = (c + 1) * n_sub

        def first_lane(t):
            return max(t - c * n_sub, 0) * kb

        def score_block(hd, t, m8):
            rows = slice(t * kb, (t + 1) * kb)
            lo = first_lane(t)
            k = jnp.concatenate([kn_ref[rows, hd * QK_NOPE:(hd + 1) * QK_NOPE], kr_ref[rows, :]],
                                axis=1)
            s = _dot(k, qt_ref[hd * HEAD_W:(hd + 1) * HEAD_W, lo:tq])
            if t >= c * n_sub:
                s_diag = jnp.where(diag_mask, s[:, 0:kb], NEG_INF)
                s = jnp.concatenate([s_diag, s[:, kb:]], axis=1) if lo + kb < tq else s_diag
            s_ref[hd % 2, rows, lo:tq] = s
            blk_max = jnp.max(s.reshape(kb // V7X_SUBLANES, V7X_SUBLANES, tq - lo), axis=0)
            if m8 is None:
                return blk_max
            if lo == 0:
                return jnp.maximum(m8, blk_max)
            return jnp.concatenate([m8[:, 0:lo], jnp.maximum(m8[:, lo:], blk_max)], axis=1)

        def pv_block(hd, t, m, acc):
            rows = slice(t * kb, (t + 1) * kb)
            lo = first_lane(t)
            p = jnp.exp2(s_ref[hd % 2, rows, lo:tq] - m[:, lo:]).astype(_BF16)
            v_aug = jnp.concatenate([vt_ref[hd * V_HEAD:(hd + 1) * V_HEAD, rows], ones], axis=0)
            o_aug = _dot(v_aug, p)
            if acc is None:
                return o_aug
            if lo == 0:
                return acc + o_aug
            return jnp.concatenate([acc[:, 0:lo], acc[:, lo:] + o_aug], axis=1)

        m8_next = None
        for t in range(n_blocks):
            m8_next = score_block(0, t, m8_next)
        for hd in range(N_HEADS):
            m = jnp.max(m8_next, axis=0, keepdims=True)
            m8_next, acc = None, None
            for t in range(n_blocks):
                if hd + 1 < N_HEADS:
                    m8_next = score_block(hd + 1, t, m8_next)
                acc = pv_block(hd, t, m, acc)
            o_t = acc[0:V_HEAD] * (1.0 / acc[V_HEAD:V_HEAD + 1])
            ob_ref[:, hd * V_HEAD:(hd + 1) * V_HEAD] = o_t.T

    for c in range(kn_ref.shape[0] // tq):
        pl.when(i == c)(functools.partial(attend, c))

    o_ref[...] = h + _dot(ob_ref[...], wo_ref[...])


def _mla(h, pos_row, freq_col, g, w_dq, g_q, w_uq2_t, w_o, k_nope, k_rope, v_t):
    bsz, s, d = h.shape
    qlora = w_dq.shape[1]
    hk = k_nope.shape[2]
    hv = v_t.shape[1]
    tq = Q_TILE
    return pl.pallas_call(
        _mla_kernel,
        grid=(bsz, s // tq),
        in_specs=[_row_spec(tq, d), pl.BlockSpec((None, 1, tq), lambda b, i: (b, 0, i)),
                  _const_spec((QK_ROPE // 2, 1)),
                  _const_spec((1, d)), _const_spec((d, qlora)), _const_spec((1, qlora)),
                  _const_spec((N_HEADS * HEAD_W, qlora)), _const_spec((hv, d)),
                  pl.BlockSpec((None, s, hk), lambda b, i: (b, 0, 0)),
                  pl.BlockSpec((None, s, 2 * QK_ROPE), lambda b, i: (b, 0, 0)),
                  pl.BlockSpec((None, hv, s), lambda b, i: (b, 0, 0))],
        out_specs=_row_spec(tq, d),
        out_shape=jax.ShapeDtypeStruct(h.shape, _F32),
        scratch_shapes=[pltpu.VMEM((N_HEADS * HEAD_W, tq), _BF16),
                        pltpu.VMEM((tq, hv), _F32),
                        pltpu.VMEM((2, s, tq), _F32)],
        compiler_params=_params(),
        name="mla_attention",
    )(h, pos_row, freq_col, g, w_dq, g_q, w_uq2_t, w_o, k_nope, k_rope, v_t)


def _swap_halves(w):
    half = w.shape[-1] // 2
    return jnp.concatenate([w[..., half:], w[..., :half]], axis=-1)


def kernel(x, positions, attn_norm, ffn_norm, final_norm, sc_w_in, sc_conv_w, sc_w_out, kv_in_norm, w_dkv, kv_latent_norm, w_kr, w_uk, w_uv, w_dq, q_latent_norm, w_uq, w_o, ffn_w_up, ffn_conv_w, ffn_conv_b, ffn_w_down):
    depth = attn_norm.shape[0]
    n_a = sc_w_in.shape[0]
    row = lambda v: v.reshape(1, -1)

    half = QK_ROPE // 2
    inv_freq = 1.0 / (ROPE_THETA ** (jnp.arange(half, dtype=_F32) / half))
    freq_col = inv_freq.reshape(half, 1)
    pos_row = positions.reshape(positions.shape[0], 1, positions.shape[1])

    w_kr2 = jnp.concatenate([w_kr, _swap_halves(w_kr)], axis=1)
    final_g = row(final_norm)

    h = x
    kv = None
    for layer in range(depth):
        g_attn = row(attn_norm[layer])
        if layer < n_a:
            h = _mixer(h, g_attn, sc_w_in[layer], sc_conv_w[layer], sc_w_out[layer])
        else:
            a = layer - n_a
            wq = w_uq[a].reshape(-1, N_HEADS, QK_NOPE + QK_ROPE)
            rope_cols = wq[:, :, QK_NOPE:]
            w_uq2 = jnp.concatenate([wq, _swap_halves(rope_cols)], axis=2)
            w_uq2_t = w_uq2.reshape(-1, N_HEADS * HEAD_W).T
            h = _mla(h, pos_row, freq_col, g_attn, w_dq[a], row(q_latent_norm[a]), w_uq2_t,
                     w_o[a], *kv)
        h = _ffn(h, row(ffn_norm[layer]), ffn_w_up, ffn_conv_w[layer], row(ffn_conv_b[layer]),
                 ffn_w_down, final_g, layer=layer, final_norm=(layer == depth - 1))
        if layer == n_a - 1:
            kv = _shared_kv(h, pos_row, freq_col, row(kv_in_norm), w_dkv, row(kv_latent_norm),
                            w_kr2, w_uk, w_uv.T)
    return h
```

```python
import functools
import math

import jax
import jax.numpy as jnp
from jax import lax
from jax.experimental import pallas as pl
from jax.experimental.pallas import tpu as pltpu

CHUNK = 64
CONV_W = 3
N_HEADS = 8
QK_NOPE = 128
QK_ROPE = 64
V_HEAD = 128
ROPE_THETA = 10000.0
EPS = 1e-6
NEG_INF = -1e30

V7X_SUBLANES = 8
V7X_LANES = 128
V7X_MXU_DIM = 256
V7X_VMEM_BYTES = 64 * 1024 * 1024

HEAD_W = QK_NOPE + 2 * QK_ROPE
ROW_TILE = 512
KV_TILE = 1024
Q_TILE = 512
KEY_BLOCK = V7X_MXU_DIM
FF_CHUNK = V7X_MXU_DIM
VMEM_LIMIT = V7X_VMEM_BYTES - 8 * 1024 * 1024

_BF16 = jnp.bfloat16
_F32 = jnp.float32


def _dot(a, b):
    return jnp.dot(a, b, preferred_element_type=_F32)


def _rms(x, g):
    return x * lax.rsqrt(jnp.mean(x * x, axis=-1, keepdims=True) + EPS) * g


def _const_spec(shape):
    nd = len(shape)
    return pl.BlockSpec(shape, lambda *_: (0,) * nd, pipeline_mode=pl.Buffered(1))


def _row_spec(tile, width):
    return pl.BlockSpec((None, tile, width), lambda b, i: (b, i, 0))


def _cast_plan(w_up, w_down, layer, grid):
    bf16_rows = 2 * V7X_SUBLANES
    n_steps = grid[0] * grid[1]
    _, d, up_cols = w_up.shape
    _, dff, dn_cols = w_down.shape
    up_rows = d // n_steps
    every = next(k for k in range(1, n_steps + 1)
                 if n_steps % k == 0 and (dff * k) % (n_steps * bf16_rows) == 0)
    dn_rows = dff * every // n_steps
    assert up_rows * n_steps == d and up_rows % bf16_rows == 0
    step = lambda b, i: b * grid[1] + i
    in_specs = [pl.BlockSpec((None, up_rows, up_cols), lambda b, i: (layer, step(b, i), 0)),
                pl.BlockSpec((None, dn_rows, dn_cols), lambda b, i: (layer, step(b, i) // every, 0))]
    out_specs = [pl.BlockSpec((up_rows, up_cols), lambda b, i: (step(b, i), 0)),
                 pl.BlockSpec((dn_rows, dn_cols), lambda b, i: (step(b, i) // every, 0))]
    out_shape = [jax.ShapeDtypeStruct((d, up_cols), _BF16),
                 jax.ShapeDtypeStruct((dff, dn_cols), _BF16)]
    return in_specs, out_specs, out_shape


def _cast_block(src_up, src_dn, dst_up, dst_dn):
    dst_up[...] = src_up[...].astype(_BF16)
    dst_dn[...] = src_dn[...].astype(_BF16)


def _params():
    return pltpu.CompilerParams(
        dimension_semantics=("arbitrary", "arbitrary"), vmem_limit_bytes=VMEM_LIMIT)


def _causal_conv3(slab_ref, first, cur, w):
    halo, lanes = V7X_SUBLANES, slab_ref.shape[2]
    tile = cur.shape[0]
    outs = []
    for j in range(cur.shape[1] // lanes):
        cols = slice(j * lanes, (j + 1) * lanes)
        slab = slab_ref.at[first + j]
        slab[halo:halo + tile, :] = cur[:, cols]
        y = slab[pl.ds(halo - 2, tile), :] * w[0:1, cols]
        y = y + slab[pl.ds(halo - 1, tile), :] * w[1:2, cols]
        outs.append(y + cur[:, cols] * w[2:3, cols])
        slab[0:halo, :] = slab[tile:tile + halo, :]
    return jnp.concatenate(outs, axis=1)


def _mixer_kernel(x_ref, g_ref, win_ref, cw_ref, wout_ref, fup_ref, fdn_ref,
                  o_ref, fup_out, fdn_out, cu_ref):
    tile, d = x_ref.shape
    halo = V7X_SUBLANES
    _cast_block(fup_ref, fdn_ref, fup_out, fdn_out)

    @pl.when(pl.program_id(1) == 0)
    def _():
        cu_ref[:, 0:halo, :] = jnp.zeros((cu_ref.shape[0], halo, cu_ref.shape[2]), _F32)

    x = x_ref[...]
    hn = _rms(x, g_ref[...])
    b_gate = _dot(hn, win_ref[:, 0:d])
    cu = _dot(hn, win_ref[:, d:2 * d]) * _dot(hn, win_ref[:, 2 * d:3 * d])
    conv = _causal_conv3(cu_ref, 0, cu, cw_ref[...])
    o_ref[...] = x + _dot(b_gate * conv, wout_ref[...])


def _mixer(x, g, w_in, conv_w, w_out, ffn_w_up, ffn_w_down, layer):
    bsz, s, d = x.shape
    tile = ROW_TILE
    grid = (bsz, s // tile)
    cast_in, cast_out, cast_shape = _cast_plan(ffn_w_up, ffn_w_down, layer, grid)
    return pl.pallas_call(
        _mixer_kernel,
        grid=grid,
        in_specs=[_row_spec(tile, d), _const_spec((1, d)), _const_spec((d, 3 * d)),
                  _const_spec((CONV_W, d)), _const_spec((d, d))] + cast_in,
        out_specs=[_row_spec(tile, d)] + cast_out,
        out_shape=[jax.ShapeDtypeStruct(x.shape, _F32)] + cast_shape,
        scratch_shapes=[pltpu.VMEM((d // V7X_LANES, tile + V7X_SUBLANES, V7X_LANES), _F32)],
        compiler_params=_params(),
        name="sc_mixer",
    )(x, g, w_in, conv_w, w_out, ffn_w_up, ffn_w_down)


def _ffn_kernel(h_ref, g_ref, wup_ref, cw_ref, cb_ref, wdn_ref, fg_ref, o_ref,
                gbuf_ref, act_ref, *, final_norm):
    tile, d = h_ref.shape
    dff = act_ref.shape[1]
    halo = V7X_SUBLANES
    ch = FF_CHUNK

    @pl.when(pl.program_id(1) == 0)
    def _():
        gbuf_ref[:, 0:halo, :] = jnp.zeros((gbuf_ref.shape[0], halo, gbuf_ref.shape[2]), _F32)

    h = h_ref[...]
    hn = _rms(h, g_ref[...]).astype(_BF16)
    for c in range(dff // ch):
        cols = slice(c * ch, (c + 1) * ch)
        g = _dot(hn, wup_ref[:, cols])
        v = _dot(hn, wup_ref[:, dff + c * ch:dff + (c + 1) * ch])
        pre = _causal_conv3(gbuf_ref, c, g, cw_ref[:, cols]) + cb_ref[:, cols]
        act = pre * (1.0 / (1.0 + jnp.exp(-pre))) * v
        act_ref[:, cols] = act.astype(_BF16)
    out = h + _dot(act_ref[...], wdn_ref[...])
    if final_norm:
        out = _rms(out, fg_ref[...])
    o_ref[...] = out


def _ffn(h, g, w_up, conv_w, conv_b, w_down, final_g, *, final_norm):
    bsz, s, d = h.shape
    dff = w_down.shape[0]
    tile = ROW_TILE
    return pl.pallas_call(
        functools.partial(_ffn_kernel, final_norm=final_norm),
        grid=(bsz, s // tile),
        in_specs=[_row_spec(tile, d), _const_spec((1, d)), _const_spec((d, 2 * dff)),
                  _const_spec((CONV_W, dff)), _const_spec((1, dff)), _const_spec((dff, d)),
                  _const_spec((1, d))],
        out_specs=_row_spec(tile, d),
        out_shape=jax.ShapeDtypeStruct(h.shape, _F32),
        scratch_shapes=[pltpu.VMEM((dff // FF_CHUNK, tile + V7X_SUBLANES, FF_CHUNK), _F32),
                        pltpu.VMEM((tile, dff), _BF16)],
        compiler_params=_params(),
        name="conv_ffn_final" if final_norm else "conv_ffn",
    )(h, g, w_up, conv_w, conv_b, w_down, final_g)


def _rope_table_t(pos_row, inv_freq_col, scale):
    ang = pos_row.astype(_F32) * inv_freq_col
    cos, sin = jnp.cos(ang), jnp.sin(ang)
    table = jnp.concatenate([cos, cos, -sin, sin], axis=0)
    return table * scale if scale != 1.0 else table


def _kv_kernel(h_ref, pos_ref, freq_ref, gin_ref, wdkv_ref, glat_ref, wkr_ref, wuk_ref,
               wuvt_ref, kn_ref, kr_ref, vt_ref):
    hn = _rms(h_ref[...], gin_ref[...])
    c_kv = _rms(_dot(hn, wdkv_ref[...]), glat_ref[...])
    p = _dot(hn, wkr_ref[...]) * _rope_table_t(pos_ref[...], freq_ref[...], 1.0).T
    kr_ref[...] = (p + pltpu.roll(p, QK_ROPE, 1)).astype(_BF16)
    kn_ref[...] = _dot(c_kv, wuk_ref[...]).astype(_BF16)
    vt_ref[...] = _dot(wuvt_ref[...], c_kv.T).astype(_BF16)


def _shared_kv(h, pos, freq, g_in, w_dkv, g_lat, w_kr2, w_uk, w_uv_t):
    bsz, s, d = h.shape
    lora = w_dkv.shape[1]
    hk = w_uk.shape[1]
    hv = w_uv_t.shape[0]
    tile = KV_TILE
    return pl.pallas_call(
        _kv_kernel,
        grid=(bsz, s // tile),
        in_specs=[_row_spec(tile, d), pl.BlockSpec((None, 1, tile), lambda b, i: (b, 0, i)),
                  _const_spec((QK_ROPE // 2, 1)),
                  _const_spec((1, d)), _const_spec((d, lora)), _const_spec((1, lora)),
                  _const_spec((d, 2 * QK_ROPE)), _const_spec((lora, hk)),
                  _const_spec((hv, lora))],
        out_specs=[_row_spec(tile, hk), _row_spec(tile, 2 * QK_ROPE),
                   pl.BlockSpec((None, hv, tile), lambda b, i: (b, 0, i))],
        out_shape=[jax.ShapeDtypeStruct((bsz, s, hk), _BF16),
                   jax.ShapeDtypeStruct((bsz, s, 2 * QK_ROPE), _BF16),
                   jax.ShapeDtypeStruct((bsz, hv, s), _BF16)],
        compiler_params=_params(),
        name="shared_kv",
    )(h, pos, freq, g_in, w_dkv, g_lat, w_kr2, w_uk, w_uv_t)


def _mla_kernel(h_ref, posr_ref, freqc_ref, g_ref, wdq_ref, gq_ref, wuqt_ref, wo_ref,
                kn_ref, kr_ref, vt_ref, fup_ref, fdn_ref, o_ref, fup_out, fdn_out,
                qt_ref, ob_ref, s_ref):
    tq, d = h_ref.shape
    i = pl.program_id(1)
    _cast_block(fup_ref, fdn_ref, fup_out, fdn_out)
    scale = float(QK_NOPE + QK_ROPE) ** -0.5 * math.log2(math.e)

    h = h_ref[...]
    hn = _rms(h, g_ref[...])
    c_q = _rms(_dot(hn, wdq_ref[...]), gq_ref[...])
    c_qt = c_q.T
    table = _rope_table_t(posr_ref[...], freqc_ref[...], scale)
    for hd in range(N_HEADS):
        qt = _dot(wuqt_ref[hd * HEAD_W:(hd + 1) * HEAD_W, :], c_qt)
        qt_ref[hd * HEAD_W:hd * HEAD_W + QK_NOPE, :] = (qt[0:QK_NOPE] * scale).astype(_BF16)
        qt_ref[hd * HEAD_W + QK_NOPE:(hd + 1) * HEAD_W, :] = (qt[QK_NOPE:] * table).astype(_BF16)

    ones_rows = 2 * V7X_SUBLANES
    kb = KEY_BLOCK
    n_sub = tq // kb
    k_chunk = lax.broadcasted_iota(jnp.int32, (kb, kb), 0) // CHUNK
    q_chunk = lax.broadcasted_iota(jnp.int32, (kb, kb), 1) // CHUNK
    diag_mask = k_chunk <= q_chunk
    ones = jnp.ones((ones_rows, kb), _BF16)

    def attend(c):
        n_blocks = (c + 1) * n_sub

        def first_lane(t):
            return max(t - c * n_sub, 0) * kb

        def score_block(hd, t, m8):
            rows = slice(t * kb, (t + 1) * kb)
            lo = first_lane(t)
            k = jnp.concatenate([kn_ref[rows, hd * QK_NOPE:(hd + 1) * QK_NOPE], kr_ref[rows, :]],
                                axis=1)
            s = _dot(k, qt_ref[hd * HEAD_W:(hd + 1) * HEAD_W, lo:tq])
            if t >= c * n_sub:
                s_diag = jnp.where(diag_mask, s[:, 0:kb], NEG_INF)
                s = jnp.concatenate([s_diag, s[:, kb:]], axis=1) if lo + kb < tq else s_diag
            s_ref[hd % 2, rows, lo:tq] = s
            blk_max = jnp.max(s.reshape(kb // V7X_SUBLANES, V7X_SUBLANES, tq - lo), axis=0)
            if m8 is None:
                return blk_max
            if lo == 0:
                return jnp.maximum(m8, blk_max)
            return jnp.concatenate([m8[:, 0:lo], jnp.maximum(m8[:, lo:], blk_max)], axis=1)

        def pv_block(hd, t, m, acc):
            rows = slice(t * kb, (t + 1) * kb)
            lo = first_lane(t)
            p = jnp.exp2(s_ref[hd % 2, rows, lo:tq] - m[:, lo:]).astype(_BF16)
            v_aug = jnp.concatenate([vt_ref[hd * V_HEAD:(hd + 1) * V_HEAD, rows], ones], axis=0)
            o_aug = _dot(v_aug, p)
            if acc is None:
                return o_aug
            if lo == 0:
                return acc + o_aug
            return jnp.concatenate([acc[:, 0:lo], acc[:, lo:] + o_aug], axis=1)

        m8_next = None
        for t in range(n_blocks):
            m8_next = score_block(0, t, m8_next)
        for hd in range(N_HEADS):
            m = jnp.max(m8_next, axis=0, keepdims=True)
            m8_next, acc = None, None
            for t in range(n_blocks):
                if hd + 1 < N_HEADS:
                    m8_next = score_block(hd + 1, t, m8_next)
                acc = pv_block(hd, t, m, acc)
            o_t = acc[0:V_HEAD] * (1.0 / acc[V_HEAD:V_HEAD + 1])
            ob_ref[:, hd * V_HEAD:(hd + 1) * V_HEAD] = o_t.T

    for c in range(kn_ref.shape[0] // tq):
        pl.when(i == c)(functools.partial(attend, c))

    o_ref[...] = h + _dot(ob_ref[...], wo_ref[...])


def _mla(h, pos_row, freq_col, g, w_dq, g_q, w_uq2_t, w_o, k_nope, k_rope, v_t,
         ffn_w_up, ffn_w_down, layer):
    bsz, s, d = h.shape
    qlora = w_dq.shape[1]
    hk = k_nope.shape[2]
    hv = v_t.shape[1]
    tq = Q_TILE
    grid = (bsz, s // tq)
    cast_in, cast_out, cast_shape = _cast_plan(ffn_w_up, ffn_w_down, layer, grid)
    return pl.pallas_call(
        _mla_kernel,
        grid=grid,
        in_specs=[_row_spec(tq, d), pl.BlockSpec((None, 1, tq), lambda b, i: (b, 0, i)),
                  _const_spec((QK_ROPE // 2, 1)),
                  _const_spec((1, d)), _const_spec((d, qlora)), _const_spec((1, qlora)),
                  _const_spec((N_HEADS * HEAD_W, qlora)), _const_spec((hv, d)),
                  pl.BlockSpec((None, s, hk), lambda b, i: (b, 0, 0)),
                  pl.BlockSpec((None, s, 2 * QK_ROPE), lambda b, i: (b, 0, 0)),
                  pl.BlockSpec((None, hv, s), lambda b, i: (b, 0, 0))] + cast_in,
        out_specs=[_row_spec(tq, d)] + cast_out,
        out_shape=[jax.ShapeDtypeStruct(h.shape, _F32)] + cast_shape,
        scratch_shapes=[pltpu.VMEM((N_HEADS * HEAD_W, tq), _BF16),
                        pltpu.VMEM((tq, hv), _F32),
                        pltpu.VMEM((2, s, tq), _F32)],
        compiler_params=_params(),
        name="mla_attention",
    )(h, pos_row, freq_col, g, w_dq, g_q, w_uq2_t, w_o, k_nope, k_rope, v_t,
      ffn_w_up, ffn_w_down)


def _swap_halves(w):
    half = w.shape[-1] // 2
    return jnp.concatenate([w[..., half:], w[..., :half]], axis=-1)


def kernel(x, positions, attn_norm, ffn_norm, final_norm, sc_w_in, sc_conv_w, sc_w_out, kv_in_norm, w_dkv, kv_latent_norm, w_kr, w_uk, w_uv, w_dq, q_latent_norm, w_uq, w_o, ffn_w_up, ffn_conv_w, ffn_conv_b, ffn_w_down):
    depth = attn_norm.shape[0]
    n_a = sc_w_in.shape[0]
    row = lambda v: v.reshape(1, -1)

    half = QK_ROPE // 2
    inv_freq = 1.0 / (ROPE_THETA ** (jnp.arange(half, dtype=_F32) / half))
    freq_col = inv_freq.reshape(half, 1)
    pos_row = positions.reshape(positions.shape[0], 1, positions.shape[1])

    w_kr2 = jnp.concatenate([w_kr, _swap_halves(w_kr)], axis=1)
    final_g = row(final_norm)

    h = x
    kv = None
    for layer in range(depth):
        g_attn = row(attn_norm[layer])
        if layer < n_a:
            h, w_up, w_down = _mixer(h, g_attn, sc_w_in[layer], sc_conv_w[layer], sc_w_out[layer],
                                     ffn_w_up, ffn_w_down, layer)
        else:
            a = layer - n_a
            wq = w_uq[a].reshape(-1, N_HEADS, QK_NOPE + QK_ROPE)
            rope_cols = wq[:, :, QK_NOPE:]
            w_uq2 = jnp.concatenate([wq, _swap_halves(rope_cols)], axis=2)
            w_uq2_t = w_uq2.reshape(-1, N_HEADS * HEAD_W).T
            h, w_up, w_down = _mla(h, pos_row, freq_col, g_attn, w_dq[a], row(q_latent_norm[a]),
                                   w_uq2_t, w_o[a], *kv, ffn_w_up, ffn_w_down, layer)
        h = _ffn(h, row(ffn_norm[layer]), w_up, ffn_conv_w[layer], row(ffn_conv_b[layer]),
                 w_down, final_g, final_norm=(layer == depth - 1))
        if layer == n_a - 1:
            kv = _shared_kv(h, pos_row, freq_col, row(kv_in_norm), w_dkv, row(kv_latent_norm),
                            w_kr2, w_uk, w_uv.T)
    return h
```

```python
import functools
import math

import jax
import jax.numpy as jnp
from jax import lax
from jax.experimental import pallas as pl
from jax.experimental.pallas import tpu as pltpu

CHUNK = 64
CONV_W = 3
N_HEADS = 8
QK_NOPE = 128
QK_ROPE = 64
V_HEAD = 128
ROPE_THETA = 10000.0
EPS = 1e-6
NEG_INF = -1e30

V7X_SUBLANES = 8
V7X_LANES = 128
V7X_MXU_DIM = 256
V7X_VMEM_BYTES = 64 * 1024 * 1024

HEAD_W = QK_NOPE + 2 * QK_ROPE
ROW_TILE = 512
KV_TILE = 1024
Q_TILE = 512
KEY_BLOCK = V7X_MXU_DIM
FF_CHUNK = V7X_MXU_DIM
VMEM_LIMIT = V7X_VMEM_BYTES - 8 * 1024 * 1024

_BF16 = jnp.bfloat16
_F32 = jnp.float32


def _dot(a, b):
    return jnp.dot(a, b, preferred_element_type=_F32)


def _rms(x, g):
    return x * lax.rsqrt(jnp.mean(x * x, axis=-1, keepdims=True) + EPS) * g


def _const_spec(shape):
    nd = len(shape)
    return pl.BlockSpec(shape, lambda *_: (0,) * nd, pipeline_mode=pl.Buffered(1))


def _layer_spec(shape, layer):
    nd = len(shape)
    return pl.BlockSpec((None,) + tuple(shape), lambda *_: (layer,) + (0,) * nd,
                        pipeline_mode=pl.Buffered(1))


def _row_spec(tile, width):
    return pl.BlockSpec((None, tile, width), lambda b, i: (b, i, 0))


def _cast_plan(w_up, w_down, layer, grid):
    bf16_rows = 2 * V7X_SUBLANES
    n_steps = grid[0] * grid[1]
    _, d, up_cols = w_up.shape
    _, dff, dn_cols = w_down.shape
    up_rows = d // n_steps
    every = next(k for k in range(1, n_steps + 1)
                 if n_steps % k == 0 and (dff * k) % (n_steps * bf16_rows) == 0)
    dn_rows = dff * every // n_steps
    assert up_rows * n_steps == d and up_rows % bf16_rows == 0
    step = lambda b, i: b * grid[1] + i
    in_specs = [pl.BlockSpec((None, up_rows, up_cols), lambda b, i: (layer, step(b, i), 0)),
                pl.BlockSpec((None, dn_rows, dn_cols), lambda b, i: (layer, step(b, i) // every, 0))]
    out_specs = [pl.BlockSpec((up_rows, up_cols), lambda b, i: (step(b, i), 0)),
                 pl.BlockSpec((dn_rows, dn_cols), lambda b, i: (step(b, i) // every, 0))]
    out_shape = [jax.ShapeDtypeStruct((d, up_cols), _BF16),
                 jax.ShapeDtypeStruct((dff, dn_cols), _BF16)]
    return in_specs, out_specs, out_shape


def _cast_block(src_up, src_dn, dst_up, dst_dn):
    dst_up[...] = src_up[...].astype(_BF16)
    dst_dn[...] = src_dn[...].astype(_BF16)


def _params():
    return pltpu.CompilerParams(
        dimension_semantics=("arbitrary", "arbitrary"), vmem_limit_bytes=VMEM_LIMIT)


def _causal_conv3(slab_ref, first, cur, w):
    halo, lanes = V7X_SUBLANES, slab_ref.shape[2]
    tile = cur.shape[0]
    outs = []
    for j in range(cur.shape[1] // lanes):
        cols = slice(j * lanes, (j + 1) * lanes)
        slab = slab_ref.at[first + j]
        slab[halo:halo + tile, :] = cur[:, cols]
        y = slab[pl.ds(halo - 2, tile), :] * w[0:1, cols]
        y = y + slab[pl.ds(halo - 1, tile), :] * w[1:2, cols]
        outs.append(y + cur[:, cols] * w[2:3, cols])
        slab[0:halo, :] = slab[tile:tile + halo, :]
    return jnp.concatenate(outs, axis=1)


def _mixer_kernel(x_ref, g_ref, win_ref, cw_ref, wout_ref, fup_ref, fdn_ref,
                  o_ref, fup_out, fdn_out, cu_ref):
    tile, d = x_ref.shape
    halo = V7X_SUBLANES
    _cast_block(fup_ref, fdn_ref, fup_out, fdn_out)

    @pl.when(pl.program_id(1) == 0)
    def _():
        cu_ref[:, 0:halo, :] = jnp.zeros((cu_ref.shape[0], halo, cu_ref.shape[2]), _F32)

    x = x_ref[...]
    hn = _rms(x, g_ref[...])
    b_gate = _dot(hn, win_ref[:, 0:d])
    cu = _dot(hn, win_ref[:, d:2 * d]) * _dot(hn, win_ref[:, 2 * d:3 * d])
    conv = _causal_conv3(cu_ref, 0, cu, cw_ref[...])
    o_ref[...] = x + _dot(b_gate * conv, wout_ref[...])


def _mixer(x, g, w_in, conv_w, w_out, ffn_w_up, ffn_w_down, layer, a):
    bsz, s, d = x.shape
    tile = ROW_TILE
    grid = (bsz, s // tile)
    cast_in, cast_out, cast_shape = _cast_plan(ffn_w_up, ffn_w_down, layer, grid)
    return pl.pallas_call(
        _mixer_kernel,
        grid=grid,
        in_specs=[_row_spec(tile, d), _layer_spec((1, d), layer), _layer_spec((d, 3 * d), a),
                  _layer_spec((CONV_W, d), a), _layer_spec((d, d), a)] + cast_in,
        out_specs=[_row_spec(tile, d)] + cast_out,
        out_shape=[jax.ShapeDtypeStruct(x.shape, _F32)] + cast_shape,
        scratch_shapes=[pltpu.VMEM((d // V7X_LANES, tile + V7X_SUBLANES, V7X_LANES), _F32)],
        compiler_params=_params(),
        name="sc_mixer",
    )(x, g, w_in, conv_w, w_out, ffn_w_up, ffn_w_down)


def _ffn_kernel(h_ref, g_ref, wup_ref, cw_ref, cb_ref, wdn_ref, fg_ref, o_ref,
                gbuf_ref, act_ref, *, final_norm):
    tile, d = h_ref.shape
    dff = act_ref.shape[1]
    halo = V7X_SUBLANES
    ch = FF_CHUNK

    @pl.when(pl.program_id(1) == 0)
    def _():
        gbuf_ref[:, 0:halo, :] = jnp.zeros((gbuf_ref.shape[0], halo, gbuf_ref.shape[2]), _F32)

    h = h_ref[...]
    hn = _rms(h, g_ref[...]).astype(_BF16)
    for c in range(dff // ch):
        cols = slice(c * ch, (c + 1) * ch)
        g = _dot(hn, wup_ref[:, cols])
        v = _dot(hn, wup_ref[:, dff + c * ch:dff + (c + 1) * ch])
        pre = _causal_conv3(gbuf_ref, c, g, cw_ref[:, cols]) + cb_ref[:, cols]
        act = pre * (1.0 / (1.0 + jnp.exp(-pre))) * v
        act_ref[:, cols] = act.astype(_BF16)
    out = h + _dot(act_ref[...], wdn_ref[...])
    if final_norm:
        out = _rms(out, fg_ref[...])
    o_ref[...] = out


def _ffn(h, g, w_up, conv_w, conv_b, w_down, final_g, *, layer, final_norm):
    bsz, s, d = h.shape
    dff = w_down.shape[0]
    tile = ROW_TILE
    return pl.pallas_call(
        functools.partial(_ffn_kernel, final_norm=final_norm),
        grid=(bsz, s // tile),
        in_specs=[_row_spec(tile, d), _layer_spec((1, d), layer), _const_spec((d, 2 * dff)),
                  _layer_spec((CONV_W, dff), layer), _layer_spec((1, dff), layer),
                  _const_spec((dff, d)), _const_spec((1, d))],
        out_specs=_row_spec(tile, d),
        out_shape=jax.ShapeDtypeStruct(h.shape, _F32),
        scratch_shapes=[pltpu.VMEM((dff // FF_CHUNK, tile + V7X_SUBLANES, FF_CHUNK), _F32),
                        pltpu.VMEM((tile, dff), _BF16)],
        compiler_params=_params(),
        name="conv_ffn_final" if final_norm else "conv_ffn",
    )(h, g, w_up, conv_w, conv_b, w_down, final_g)


def _rope_table_t(pos_row, inv_freq_col, scale):
    ang = pos_row.astype(_F32) * inv_freq_col
    cos, sin = jnp.cos(ang), jnp.sin(ang)
    table = jnp.concatenate([cos, cos, -sin, sin], axis=0)
    return table * scale if scale != 1.0 else table


def _kv_kernel(h_ref, pos_ref, freq_ref, gin_ref, wdkv_ref, glat_ref, wkr_ref, wuk_ref,
               wuv_ref, kn_ref, kr_ref, vt_ref):
    hn = _rms(h_ref[...], gin_ref[...])
    c_kv = _rms(_dot(hn, wdkv_ref[...]), glat_ref[...])
    p = _dot(hn, wkr_ref[...]) * _rope_table_t(pos_ref[...], freq_ref[...], 1.0).T
    kr_ref[...] = (p + pltpu.roll(p, QK_ROPE, 1)).astype(_BF16)
    kn_ref[...] = _dot(c_kv, wuk_ref[...]).astype(_BF16)
    vt_ref[...] = _dot(wuv_ref[...].T, c_kv.T).astype(_BF16)


def _shared_kv(h, pos, freq, g_in, w_dkv, g_lat, w_kr2, w_uk, w_uv):
    bsz, s, d = h.shape
    lora = w_dkv.shape[1]
    hk = w_uk.shape[1]
    hv = w_uv.shape[1]
    tile = KV_TILE
    return pl.pallas_call(
        _kv_kernel,
        grid=(bsz, s // tile),
        in_specs=[_row_spec(tile, d), pl.BlockSpec((None, 1, tile), lambda b, i: (b, 0, i)),
                  _const_spec((QK_ROPE // 2, 1)),
                  _const_spec((1, d)), _const_spec((d, lora)), _const_spec((1, lora)),
                  _const_spec((d, 2 * QK_ROPE)), _const_spec((lora, hk)),
                  _const_spec((lora, hv))],
        out_specs=[_row_spec(tile, hk), _row_spec(tile, 2 * QK_ROPE),
                   pl.BlockSpec((None, hv, tile), lambda b, i: (b, 0, i))],
        out_shape=[jax.ShapeDtypeStruct((bsz, s, hk), _BF16),
                   jax.ShapeDtypeStruct((bsz, s, 2 * QK_ROPE), _BF16),
                   jax.ShapeDtypeStruct((bsz, hv, s), _BF16)],
        compiler_params=_params(),
        name="shared_kv",
    )(h, pos, freq, g_in, w_dkv, g_lat, w_kr2, w_uk, w_uv)


def _mla_kernel(h_ref, posr_ref, freqc_ref, g_ref, wdq_ref, gq_ref, wuqt_ref, wo_ref,
                kn_ref, kr_ref, vt_ref, fup_ref, fdn_ref, o_ref, fup_out, fdn_out,
                qt_ref, ob_ref, s_ref):
    tq, d = h_ref.shape
    i = pl.program_id(1)
    _cast_block(fup_ref, fdn_ref, fup_out, fdn_out)
    scale = float(QK_NOPE + QK_ROPE) ** -0.5 * math.log2(math.e)

    h = h_ref[...]
    hn = _rms(h, g_ref[...])
    c_q = _rms(_dot(hn, wdq_ref[...]), gq_ref[...])
    c_qt = c_q.T
    table = _rope_table_t(posr_ref[...], freqc_ref[...], scale)
    for hd in range(N_HEADS):
        qt = _dot(wuqt_ref[hd * HEAD_W:(hd + 1) * HEAD_W, :], c_qt)
        qt_ref[hd * HEAD_W:hd * HEAD_W + QK_NOPE, :] = (qt[0:QK_NOPE] * scale).astype(_BF16)
        qt_ref[hd * HEAD_W + QK_NOPE:(hd + 1) * HEAD_W, :] = (qt[QK_NOPE:] * table).astype(_BF16)

    ones_rows = 2 * V7X_SUBLANES
    kb = KEY_BLOCK
    n_sub = tq // kb
    k_chunk = lax.broadcasted_iota(jnp.int32, (kb, kb), 0) // CHUNK
    q_chunk = lax.broadcasted_iota(jnp.int32, (kb, kb), 1) // CHUNK
    diag_mask = k_chunk <= q_chunk
    ones = jnp.ones((ones_rows, kb), _BF16)

    def attend(c):
        n_blocks = (c + 1) * n_sub

        def first_lane(t):
            return max(t - c * n_sub, 0) * kb

        def score_block(hd, t, m8):
            rows = slice(t * kb, (t + 1) * kb)
            lo = first_lane(t)
            k = jnp.concatenate([kn_ref[rows, hd * QK_NOPE:(hd + 1) * QK_NOPE], kr_ref[rows, :]],
                                axis=1)
            s = _dot(k, qt_ref[hd * HEAD_W:(hd + 1) * HEAD_W, lo:tq])
            if t >= c * n_sub:
                s_diag = jnp.where(diag_mask, s[:, 0:kb], NEG_INF)
                s = jnp.concatenate([s_diag, s[:, kb:]], axis=1) if lo + kb < tq else s_diag
            s_ref[hd % 2, rows, lo:tq] = s
            blk_max = jnp.max(s.reshape(kb // V7X_SUBLANES, V7X_SUBLANES, tq - lo), axis=0)
            if m8 is None:
                return blk_max
            if lo == 0:
                return jnp.maximum(m8, blk_max)
            return jnp.concatenate([m8[:, 0:lo], jnp.maximum(m8[:, lo:], blk_max)], axis=1)

        def pv_block(hd, t, m, acc):
            rows = slice(t * kb, (t + 1) * kb)
            lo = first_lane(t)
            p = jnp.exp2(s_ref[hd % 2, rows, lo:tq] - m[:, lo:]).astype(_BF16)
            v_aug = jnp.concatenate([vt_ref[hd * V_HEAD:(hd + 1) * V_HEAD, rows], ones], axis=0)
            o_aug = _dot(v_aug, p)
            if acc is None:
                return o_aug
            if lo == 0:
                return acc + o_aug
            return jnp.concatenate([acc[:, 0:lo], acc[:, lo:] + o_aug], axis=1)

        m8_next = None
        for t in range(n_blocks):
            m8_next = score_block(0, t, m8_next)
        for hd in range(N_HEADS):
            m = jnp.max(m8_next, axis=0, keepdims=True)
            m8_next, acc = None, None
            for t in range(n_blocks):
                if hd + 1 < N_HEADS:
                    m8_next = score_block(hd + 1, t, m8_next)
                acc = pv_block(hd, t, m, acc)
            o_t = acc[0:V_HEAD] * (1.0 / acc[V_HEAD:V_HEAD + 1])
            ob_ref[:, hd * V_HEAD:(hd + 1) * V_HEAD] = o_t.T

    for c in range(kn_ref.shape[0] // tq):
        pl.when(i == c)(functools.partial(attend, c))

    o_ref[...] = h + _dot(ob_ref[...], wo_ref[...])


def _mla(h, pos_row, freq_col, g, w_dq, g_q, w_uq2_t, w_o, k_nope, k_rope, v_t,
         ffn_w_up, ffn_w_down, layer, a):
    bsz, s, d = h.shape
    qlora = w_dq.shape[2]
    hk = k_nope.shape[2]
    hv = v_t.shape[1]
    tq = Q_TILE
    grid = (bsz, s // tq)
    cast_in, cast_out, cast_shape = _cast_plan(ffn_w_up, ffn_w_down, layer, grid)
    return pl.pallas_call(
        _mla_kernel,
        grid=grid,
        in_specs=[_row_spec(tq, d), pl.BlockSpec((None, 1, tq), lambda b, i: (b, 0, i)),
                  _const_spec((QK_ROPE // 2, 1)),
                  _layer_spec((1, d), layer), _layer_spec((d, qlora), a),
                  _layer_spec((1, qlora), a),
                  _const_spec((N_HEADS * HEAD_W, qlora)), _layer_spec((hv, d), a),
                  pl.BlockSpec((None, s, hk), lambda b, i: (b, 0, 0)),
                  pl.BlockSpec((None, s, 2 * QK_ROPE), lambda b, i: (b, 0, 0)),
                  pl.BlockSpec((None, hv, s), lambda b, i: (b, 0, 0))] + cast_in,
        out_specs=[_row_spec(tq, d)] + cast_out,
        out_shape=[jax.ShapeDtypeStruct(h.shape, _F32)] + cast_shape,
        scratch_shapes=[pltpu.VMEM((N_HEADS * HEAD_W, tq), _BF16),
                        pltpu.VMEM((tq, hv), _F32),
                        pltpu.VMEM((2, s, tq), _F32)],
        compiler_params=_params(),
        name="mla_attention",
    )(h, pos_row, freq_col, g, w_dq, g_q, w_uq2_t, w_o, k_nope, k_rope, v_t,
      ffn_w_up, ffn_w_down)


def _swap_halves(w):
    half = w.shape[-1] // 2
    return jnp.concatenate([w[..., half:], w[..., :half]], axis=-1)


def kernel(x, positions, attn_norm, ffn_norm, final_norm, sc_w_in, sc_conv_w, sc_w_out, kv_in_norm, w_dkv, kv_latent_norm, w_kr, w_uk, w_uv, w_dq, q_latent_norm, w_uq, w_o, ffn_w_up, ffn_conv_w, ffn_conv_b, ffn_w_down):
    depth = attn_norm.shape[0]
    n_a = sc_w_in.shape[0]
    row = lambda v: v.reshape(1, -1)

    half = QK_ROPE // 2
    inv_freq = 1.0 / (ROPE_THETA ** (jnp.arange(half, dtype=_F32) / half))
    freq_col = inv_freq.reshape(half, 1)
    pos_row = positions.reshape(positions.shape[0], 1, positions.shape[1])

    w_kr2 = jnp.concatenate([w_kr, _swap_halves(w_kr)], axis=1)
    final_g = row(final_norm)
    stack_rows = lambda v: v.reshape(v.shape[0], 1, v.shape[1])
    attn_g, ffn_g, conv_b, q_g = map(stack_rows, (attn_norm, ffn_norm, ffn_conv_b, q_latent_norm))

    h = x
    kv = None
    for layer in range(depth):
        if layer < n_a:
            h, w_up, w_down = _mixer(h, attn_g, sc_w_in, sc_conv_w, sc_w_out,
                                     ffn_w_up, ffn_w_down, layer, layer)
        else:
            a = layer - n_a
            wq = w_uq[a].reshape(-1, N_HEADS, QK_NOPE + QK_ROPE)
            rope_cols = wq[:, :, QK_NOPE:]
            w_uq2 = jnp.concatenate([wq, _swap_halves(rope_cols)], axis=2)
            w_uq2_t = w_uq2.reshape(-1, N_HEADS * HEAD_W).T
            h, w_up, w_down = _mla(h, pos_row, freq_col, attn_g, w_dq, q_g, w_uq2_t, w_o, *kv,
                                   ffn_w_up, ffn_w_down, layer, a)
        h = _ffn(h, ffn_g, w_up, ffn_conv_w, conv_b, w_down, final_g,
                 layer=layer, final_norm=(layer == depth - 1))
        if layer == n_a - 1:
            kv = _shared_kv(h, pos_row, freq_col, row(kv_in_norm), w_dkv, row(kv_latent_norm),
                            w_kr2, w_uk, w_uv)
    return h
```

```python
import functools
import math

import jax
import jax.numpy as jnp
from jax import lax
from jax.experimental import pallas as pl
from jax.experimental.pallas import tpu as pltpu

CHUNK = 64
CONV_W = 3
N_HEADS = 8
QK_NOPE = 128
QK_ROPE = 64
V_HEAD = 128
ROPE_THETA = 10000.0
EPS = 1e-6
NEG_INF = -1e30

V7X_SUBLANES = 8
V7X_LANES = 128
V7X_MXU_DIM = 256
V7X_VMEM_BYTES = 64 * 1024 * 1024

HEAD_W = QK_NOPE + 2 * QK_ROPE
ROW_TILE = 512
KV_TILE = 1024
Q_TILE = 512
KEY_BLOCK = V7X_MXU_DIM
FF_CHUNK = V7X_MXU_DIM
VMEM_LIMIT = V7X_VMEM_BYTES - 8 * 1024 * 1024

_BF16 = jnp.bfloat16
_F32 = jnp.float32


def _dot(a, b):
    return jnp.dot(a, b, preferred_element_type=_F32)


def _rms(x, g):
    return x * lax.rsqrt(jnp.mean(x * x, axis=-1, keepdims=True) + EPS) * g


def _const_spec(shape):
    nd = len(shape)
    return pl.BlockSpec(shape, lambda *_: (0,) * nd, pipeline_mode=pl.Buffered(1))


def _layer_spec(shape, layer):
    nd = len(shape)
    return pl.BlockSpec((None,) + tuple(shape), lambda *_: (layer,) + (0,) * nd,
                        pipeline_mode=pl.Buffered(1))


def _row_spec(tile, width):
    return pl.BlockSpec((None, tile, width), lambda b, i: (b, i, 0))


def _cast_plan(w_up, w_down, layer, grid):
    bf16_rows = 2 * V7X_SUBLANES
    n_steps = grid[0] * grid[1]
    _, d, up_cols = w_up.shape
    _, dff, dn_cols = w_down.shape
    up_rows = d // n_steps
    every = next(k for k in range(1, n_steps + 1)
                 if n_steps % k == 0 and (dff * k) % (n_steps * bf16_rows) == 0)
    dn_rows = dff * every // n_steps
    assert up_rows * n_steps == d and up_rows % bf16_rows == 0
    step = lambda b, i: b * grid[1] + i
    in_specs = [pl.BlockSpec((None, up_rows, up_cols), lambda b, i: (layer, step(b, i), 0)),
                pl.BlockSpec((None, dn_rows, dn_cols), lambda b, i: (layer, step(b, i) // every, 0))]
    out_specs = [pl.BlockSpec((up_rows, up_cols), lambda b, i: (step(b, i), 0)),
                 pl.BlockSpec((dn_rows, dn_cols), lambda b, i: (step(b, i) // every, 0))]
    out_shape = [jax.ShapeDtypeStruct((d, up_cols), _BF16),
                 jax.ShapeDtypeStruct((dff, dn_cols), _BF16)]
    return in_specs, out_specs, out_shape


def _cast_block(src_up, src_dn, dst_up, dst_dn):
    dst_up[...] = src_up[...].astype(_BF16)
    dst_dn[...] = src_dn[...].astype(_BF16)


def _params():
    return pltpu.CompilerParams(
        dimension_semantics=("arbitrary", "arbitrary"), vmem_limit_bytes=VMEM_LIMIT)


def _causal_conv3(slab_ref, first, cur, w):
    halo, lanes = V7X_SUBLANES, slab_ref.shape[2]
    tile = cur.shape[0]
    outs = []
    for j in range(cur.shape[1] // lanes):
        cols = slice(j * lanes, (j + 1) * lanes)
        slab = slab_ref.at[first + j]
        slab[halo:halo + tile, :] = cur[:, cols]
        y = slab[pl.ds(halo - 2, tile), :] * w[0:1, cols]
        y = y + slab[pl.ds(halo - 1, tile), :] * w[1:2, cols]
        outs.append(y + cur[:, cols] * w[2:3, cols])
        slab[0:halo, :] = slab[tile:tile + halo, :]
    return jnp.concatenate(outs, axis=1)


def _mixer_kernel(x_ref, g_ref, win_ref, cw_ref, wout_ref, fup_ref, fdn_ref,
                  o_ref, fup_out, fdn_out, cu_ref, *, layer):
    tile, d = x_ref.shape
    halo = V7X_SUBLANES
    _cast_block(fup_ref, fdn_ref, fup_out, fdn_out)

    @pl.when(pl.program_id(1) == 0)
    def _():
        cu_ref[:, 0:halo, :] = jnp.zeros((cu_ref.shape[0], halo, cu_ref.shape[2]), _F32)

    x = x_ref[...]
    hn = _rms(x, g_ref[layer:layer + 1, :])
    b_gate = _dot(hn, win_ref[:, 0:d])
    cu = _dot(hn, win_ref[:, d:2 * d]) * _dot(hn, win_ref[:, 2 * d:3 * d])
    conv = _causal_conv3(cu_ref, 0, cu, cw_ref[...])
    o_ref[...] = x + _dot(b_gate * conv, wout_ref[...])


def _mixer(x, g, w_in, conv_w, w_out, ffn_w_up, ffn_w_down, layer, a):
    bsz, s, d = x.shape
    tile = ROW_TILE
    grid = (bsz, s // tile)
    cast_in, cast_out, cast_shape = _cast_plan(ffn_w_up, ffn_w_down, layer, grid)
    return pl.pallas_call(
        functools.partial(_mixer_kernel, layer=layer),
        grid=grid,
        in_specs=[_row_spec(tile, d), _const_spec(g.shape), _layer_spec((d, 3 * d), a),
                  _layer_spec((CONV_W, d), a), _layer_spec((d, d), a)] + cast_in,
        out_specs=[_row_spec(tile, d)] + cast_out,
        out_shape=[jax.ShapeDtypeStruct(x.shape, _F32)] + cast_shape,
        scratch_shapes=[pltpu.VMEM((d // V7X_LANES, tile + V7X_SUBLANES, V7X_LANES), _F32)],
        compiler_params=_params(),
        name="sc_mixer",
    )(x, g, w_in, conv_w, w_out, ffn_w_up, ffn_w_down)


def _ffn_kernel(h_ref, g_ref, wup_ref, cw_ref, cb_ref, wdn_ref, fg_ref, o_ref,
                gbuf_ref, act_ref, *, layer, final_norm):
    tile, d = h_ref.shape
    dff = act_ref.shape[1]
    halo = V7X_SUBLANES
    ch = FF_CHUNK

    @pl.when(pl.program_id(1) == 0)
    def _():
        gbuf_ref[:, 0:halo, :] = jnp.zeros((gbuf_ref.shape[0], halo, gbuf_ref.shape[2]), _F32)

    h = h_ref[...]
    hn = _rms(h, g_ref[layer:layer + 1, :]).astype(_BF16)
    for c in range(dff // ch):
        cols = slice(c * ch, (c + 1) * ch)
        g = _dot(hn, wup_ref[:, cols])
        v = _dot(hn, wup_ref[:, dff + c * ch:dff + (c + 1) * ch])
        pre = _causal_conv3(gbuf_ref, c, g, cw_ref[:, cols]) + cb_ref[layer:layer + 1, cols]
        act = pre * (1.0 / (1.0 + jnp.exp(-pre))) * v
        act_ref[:, cols] = act.astype(_BF16)
    out = h + _dot(act_ref[...], wdn_ref[...])
    if final_norm:
        out = _rms(out, fg_ref[...])
    o_ref[...] = out


def _ffn(h, g, w_up, conv_w, conv_b, w_down, final_g, *, layer, final_norm):
    bsz, s, d = h.shape
    dff = w_down.shape[0]
    tile = ROW_TILE
    return pl.pallas_call(
        functools.partial(_ffn_kernel, layer=layer, final_norm=final_norm),
        grid=(bsz, s // tile),
        in_specs=[_row_spec(tile, d), _const_spec(g.shape), _const_spec((d, 2 * dff)),
                  _layer_spec((CONV_W, dff), layer), _const_spec(conv_b.shape),
                  _const_spec((dff, d)), _const_spec((1, d))],
        out_specs=_row_spec(tile, d),
        out_shape=jax.ShapeDtypeStruct(h.shape, _F32),
        scratch_shapes=[pltpu.VMEM((dff // FF_CHUNK, tile + V7X_SUBLANES, FF_CHUNK), _F32),
                        pltpu.VMEM((tile, dff), _BF16)],
        compiler_params=_params(),
        name="conv_ffn_final" if final_norm else "conv_ffn",
    )(h, g, w_up, conv_w, conv_b, w_down, final_g)


def _rope_table_t(pos_row, inv_freq_col, scale):
    ang = pos_row.astype(_F32) * inv_freq_col
    cos, sin = jnp.cos(ang), jnp.sin(ang)
    table = jnp.concatenate([cos, cos, -sin, sin], axis=0)
    return table * scale if scale != 1.0 else table


def _kv_kernel(h_ref, pos_ref, freq_ref, gin_ref, wdkv_ref, glat_ref, wkr_ref, wuk_ref,
               wuv_ref, kn_ref, kr_ref, vt_ref):
    hn = _rms(h_ref[...], gin_ref[...])
    c_kv = _rms(_dot(hn, wdkv_ref[...]), glat_ref[...])
    p = _dot(hn, wkr_ref[...]) * _rope_table_t(pos_ref[...], freq_ref[...], 1.0).T
    kr_ref[...] = (p + pltpu.roll(p, QK_ROPE, 1)).astype(_BF16)
    kn_ref[...] = _dot(c_kv, wuk_ref[...]).astype(_BF16)
    vt_ref[...] = _dot(wuv_ref[...].T, c_kv.T).astype(_BF16)


def _shared_kv(h, pos, freq, g_in, w_dkv, g_lat, w_kr2, w_uk, w_uv):
    bsz, s, d = h.shape
    lora = w_dkv.shape[1]
    hk = w_uk.shape[1]
    hv = w_uv.shape[1]
    tile = KV_TILE
    return pl.pallas_call(
        _kv_kernel,
        grid=(bsz, s // tile),
        in_specs=[_row_spec(tile, d), pl.BlockSpec((None, 1, tile), lambda b, i: (b, 0, i)),
                  _const_spec((QK_ROPE // 2, 1)),
                  _const_spec((1, d)), _const_spec((d, lora)), _const_spec((1, lora)),
                  _const_spec((d, 2 * QK_ROPE)), _const_spec((lora, hk)),
                  _const_spec((lora, hv))],
        out_specs=[_row_spec(tile, hk), _row_spec(tile, 2 * QK_ROPE),
                   pl.BlockSpec((None, hv, tile), lambda b, i: (b, 0, i))],
        out_shape=[jax.ShapeDtypeStruct((bsz, s, hk), _BF16),
                   jax.ShapeDtypeStruct((bsz, s, 2 * QK_ROPE), _BF16),
                   jax.ShapeDtypeStruct((bsz, hv, s), _BF16)],
        compiler_params=_params(),
        name="shared_kv",
    )(h, pos, freq, g_in, w_dkv, g_lat, w_kr2, w_uk, w_uv)


def _mla_kernel(h_ref, posr_ref, freqc_ref, g_ref, wdq_ref, gq_ref, wuqt_ref, wo_ref,
                kn_ref, kr_ref, vt_ref, fup_ref, fdn_ref, o_ref, fup_out, fdn_out,
                qt_ref, ob_ref, s_ref, *, layer, a):
    tq, d = h_ref.shape
    i = pl.program_id(1)
    _cast_block(fup_ref, fdn_ref, fup_out, fdn_out)
    scale = float(QK_NOPE + QK_ROPE) ** -0.5 * math.log2(math.e)

    h = h_ref[...]
    hn = _rms(h, g_ref[layer:layer + 1, :])
    c_q = _rms(_dot(hn, wdq_ref[...]), gq_ref[a:a + 1, :])
    c_qt = c_q.T
    table = _rope_table_t(posr_ref[...], freqc_ref[...], scale)
    for hd in range(N_HEADS):
        qt = _dot(wuqt_ref[hd * HEAD_W:(hd + 1) * HEAD_W, :], c_qt)
        qt_ref[hd * HEAD_W:hd * HEAD_W + QK_NOPE, :] = (qt[0:QK_NOPE] * scale).astype(_BF16)
        qt_ref[hd * HEAD_W + QK_NOPE:(hd + 1) * HEAD_W, :] = (qt[QK_NOPE:] * table).astype(_BF16)

    ones_rows = 2 * V7X_SUBLANES
    kb = KEY_BLOCK
    n_sub = tq // kb
    k_chunk = lax.broadcasted_iota(jnp.int32, (kb, kb), 0) // CHUNK
    q_chunk = lax.broadcasted_iota(jnp.int32, (kb, kb), 1) // CHUNK
    diag_mask = k_chunk <= q_chunk
    ones = jnp.ones((ones_rows, kb), _BF16)

    def attend(c):
        n_blocks = (c + 1) * n_sub

        def first_lane(t):
            return max(t - c * n_sub, 0) * kb

        def score_block(hd, t, m8):
            rows = slice(t * kb, (t + 1) * kb)
            lo = first_lane(t)
            k = jnp.concatenate([kn_ref[rows, hd * QK_NOPE:(hd + 1) * QK_NOPE], kr_ref[rows, :]],
                                axis=1)
            s = _dot(k, qt_ref[hd * HEAD_W:(hd + 1) * HEAD_W, lo:tq])
            if t >= c * n_sub:
                s_diag = jnp.where(diag_mask, s[:, 0:kb], NEG_INF)
                s = jnp.concatenate([s_diag, s[:, kb:]], axis=1) if lo + kb < tq else s_diag
            s_ref[hd % 2, rows, lo:tq] = s
            blk_max = jnp.max(s.reshape(kb // V7X_SUBLANES, V7X_SUBLANES, tq - lo), axis=0)
            if m8 is None:
                return blk_max
            if lo == 0:
                return jnp.maximum(m8, blk_max)
            return jnp.concatenate([m8[:, 0:lo], jnp.maximum(m8[:, lo:], blk_max)], axis=1)

        def pv_block(hd, t, m, acc):
            rows = slice(t * kb, (t + 1) * kb)
            lo = first_lane(t)
            p = jnp.exp2(s_ref[hd % 2, rows, lo:tq] - m[:, lo:]).astype(_BF16)
            v_aug = jnp.concatenate([vt_ref[hd * V_HEAD:(hd + 1) * V_HEAD, rows], ones], axis=0)
            o_aug = _dot(v_aug, p)
            if acc is None:
                return o_aug
            if lo == 0:
                return acc + o_aug
            return jnp.concatenate([acc[:, 0:lo], acc[:, lo:] + o_aug], axis=1)

        m8_next = None
        for t in range(n_blocks):
            m8_next = score_block(0, t, m8_next)
        for hd in range(N_HEADS):
            m = jnp.max(m8_next, axis=0, keepdims=True)
            m8_next, acc = None, None
            for t in range(n_blocks):
                if hd + 1 < N_HEADS:
                    m8_next = score_block(hd + 1, t, m8_next)
                acc = pv_block(hd, t, m, acc)
            o_t = acc[0:V_HEAD] * (1.0 / acc[V_HEAD:V_HEAD + 1])
            ob_ref[:, hd * V_HEAD:(hd + 1) * V_HEAD] = o_t.T

    for c in range(kn_ref.shape[0] // tq):
        pl.when(i == c)(functools.partial(attend, c))

    o_ref[...] = h + _dot(ob_ref[...], wo_ref[...])


def _mla(h, pos_row, freq_col, g, w_dq, g_q, w_uq2_t, w_o, k_nope, k_rope, v_t,
         ffn_w_up, ffn_w_down, layer, a):
    bsz, s, d = h.shape
    qlora = w_dq.shape[2]
    hk = k_nope.shape[2]
    hv = v_t.shape[1]
    tq = Q_TILE
    grid = (bsz, s // tq)
    cast_in, cast_out, cast_shape = _cast_plan(ffn_w_up, ffn_w_down, layer, grid)
    return pl.pallas_call(
        functools.partial(_mla_kernel, layer=layer, a=a),
        grid=grid,
        in_specs=[_row_spec(tq, d), pl.BlockSpec((None, 1, tq), lambda b, i: (b, 0, i)),
                  _const_spec((QK_ROPE // 2, 1)),
                  _const_spec(g.shape), _layer_spec((d, qlora), a), _const_spec(g_q.shape),
                  _const_spec((N_HEADS * HEAD_W, qlora)), _layer_spec((hv, d), a),
                  pl.BlockSpec((None, s, hk), lambda b, i: (b, 0, 0)),
                  pl.BlockSpec((None, s, 2 * QK_ROPE), lambda b, i: (b, 0, 0)),
                  pl.BlockSpec((None, hv, s), lambda b, i: (b, 0, 0))] + cast_in,
        out_specs=[_row_spec(tq, d)] + cast_out,
        out_shape=[jax.ShapeDtypeStruct(h.shape, _F32)] + cast_shape,
        scratch_shapes=[pltpu.VMEM((N_HEADS * HEAD_W, tq), _BF16),
                        pltpu.VMEM((tq, hv), _F32),
                        pltpu.VMEM((2, s, tq), _F32)],
        compiler_params=_params(),
        name="mla_attention",
    )(h, pos_row, freq_col, g, w_dq, g_q, w_uq2_t, w_o, k_nope, k_rope, v_t,
      ffn_w_up, ffn_w_down)


def _swap_halves(w):
    half = w.shape[-1] // 2
    return jnp.concatenate([w[..., half:], w[..., :half]], axis=-1)


def kernel(x, positions, attn_norm, ffn_norm, final_norm, sc_w_in, sc_conv_w, sc_w_out, kv_in_norm, w_dkv, kv_latent_norm, w_kr, w_uk, w_uv, w_dq, q_latent_norm, w_uq, w_o, ffn_w_up, ffn_conv_w, ffn_conv_b, ffn_w_down):
    depth = attn_norm.shape[0]
    n_a = sc_w_in.shape[0]
    row = lambda v: v.reshape(1, -1)

    half = QK_ROPE // 2
    inv_freq = 1.0 / (ROPE_THETA ** (jnp.arange(half, dtype=_F32) / half))
    freq_col = inv_freq.reshape(half, 1)
    pos_row = positions.reshape(positions.shape[0], 1, positions.shape[1])

    w_kr2 = jnp.concatenate([w_kr, _swap_halves(w_kr)], axis=1)
    final_g = row(final_norm)

    h = x
    kv = None
    for layer in range(depth):
        if layer < n_a:
            h, w_up, w_down = _mixer(h, attn_norm, sc_w_in, sc_conv_w, sc_w_out,
                                     ffn_w_up, ffn_w_down, layer, layer)
        else:
            a = layer - n_a
            wq = w_uq[a].reshape(-1, N_HEADS, QK_NOPE + QK_ROPE)
            rope_cols = wq[:, :, QK_NOPE:]
            w_uq2 = jnp.concatenate([wq, _swap_halves(rope_cols)], axis=2)
            w_uq2_t = w_uq2.reshape(-1, N_HEADS * HEAD_W).T
            h, w_up, w_down = _mla(h, pos_row, freq_col, attn_norm, w_dq, q_latent_norm,
                                   w_uq2_t, w_o, *kv, ffn_w_up, ffn_w_down, layer, a)
        h = _ffn(h, ffn_norm, w_up, ffn_conv_w, ffn_conv_b, w_down, final_g,
                 layer=layer, final_norm=(layer == depth - 1))
        if layer == n_a - 1:
            kv = _shared_kv(h, pos_row, freq_col, row(kv_in_norm), w_dkv, row(kv_latent_norm),
                            w_kr2, w_uk, w_uv)
    return h
```

```python
import functools
import math

import jax
import jax.numpy as jnp
from jax import lax
from jax.experimental import pallas as pl
from jax.experimental.pallas import tpu as pltpu

CHUNK = 64
CONV_W = 3
N_HEADS = 8
QK_NOPE = 128
QK_ROPE = 64
V_HEAD = 128
ROPE_THETA = 10000.0
EPS = 1e-6
NEG_INF = -1e30

V7X_SUBLANES = 8
V7X_LANES = 128
V7X_MXU_DIM = 256
V7X_VMEM_BYTES = 64 * 1024 * 1024

HEAD_W = QK_NOPE + 2 * QK_ROPE
ROW_TILE = 512
KV_TILE = 1024
Q_TILE = 512
KEY_BLOCK = V7X_MXU_DIM
FF_CHUNK = V7X_MXU_DIM
FF_STEP = 2 * FF_CHUNK
VMEM_LIMIT = V7X_VMEM_BYTES - 8 * 1024 * 1024

_BF16 = jnp.bfloat16
_F32 = jnp.float32


def _dot(a, b):
    return jnp.dot(a, b, preferred_element_type=_F32)


def _rms(x, g):
    return x * lax.rsqrt(jnp.mean(x * x, axis=-1, keepdims=True) + EPS) * g


def _const_spec(shape):
    nd = len(shape)
    return pl.BlockSpec(shape, lambda *_: (0,) * nd, pipeline_mode=pl.Buffered(1))


def _layer_spec(shape, layer):
    nd = len(shape)
    return pl.BlockSpec((None,) + tuple(shape), lambda *_: (layer,) + (0,) * nd,
                        pipeline_mode=pl.Buffered(1))


def _row_spec(tile, width):
    return pl.BlockSpec((None, tile, width), lambda b, i: (b, i, 0))


def _cast_plan(w_up, w_down, layer, grid):
    bf16_rows = 2 * V7X_SUBLANES
    n_steps = grid[0] * grid[1]
    _, d, up_cols = w_up.shape
    _, dff, dn_cols = w_down.shape
    up_rows = d // n_steps
    every = next(k for k in range(1, n_steps + 1)
                 if n_steps % k == 0 and (dff * k) % (n_steps * bf16_rows) == 0)
    dn_rows = dff * every // n_steps
    assert up_rows * n_steps == d and up_rows % bf16_rows == 0
    step = lambda b, i: b * grid[1] + i
    in_specs = [pl.BlockSpec((None, up_rows, up_cols), lambda b, i: (layer, step(b, i), 0)),
                pl.BlockSpec((None, dn_rows, dn_cols), lambda b, i: (layer, step(b, i) // every, 0))]
    out_specs = [pl.BlockSpec((up_rows, up_cols), lambda b, i: (step(b, i), 0)),
                 pl.BlockSpec((dn_rows, dn_cols), lambda b, i: (step(b, i) // every, 0))]
    out_shape = [jax.ShapeDtypeStruct((d, up_cols), _BF16),
                 jax.ShapeDtypeStruct((dff, dn_cols), _BF16)]
    return in_specs, out_specs, out_shape


def _cast_block(src_up, src_dn, dst_up, dst_dn):
    dst_up[...] = src_up[...].astype(_BF16)
    dst_dn[...] = src_dn[...].astype(_BF16)


def _params():
    return pltpu.CompilerParams(
        dimension_semantics=("arbitrary", "arbitrary"), vmem_limit_bytes=VMEM_LIMIT)


def _causal_conv3(slab_ref, first, cur, w):
    halo, lanes = V7X_SUBLANES, slab_ref.shape[2]
    tile = cur.shape[0]
    outs = []
    for j in range(cur.shape[1] // lanes):
        cols = slice(j * lanes, (j + 1) * lanes)
        slab = slab_ref.at[first + j]
        slab[halo:halo + tile, :] = cur[:, cols]
        y = slab[pl.ds(halo - 2, tile), :] * w[0:1, cols]
        y = y + slab[pl.ds(halo - 1, tile), :] * w[1:2, cols]
        outs.append(y + cur[:, cols] * w[2:3, cols])
        slab[0:halo, :] = slab[tile:tile + halo, :]
    return jnp.concatenate(outs, axis=1)


def _mixer_kernel(x_ref, g_ref, win_ref, cw_ref, wout_ref, fup_ref, fdn_ref,
                  o_ref, fup_out, fdn_out, cu_ref, *, layer):
    tile, d = x_ref.shape
    halo = V7X_SUBLANES
    _cast_block(fup_ref, fdn_ref, fup_out, fdn_out)

    @pl.when(pl.program_id(1) == 0)
    def _():
        cu_ref[:, 0:halo, :] = jnp.zeros((cu_ref.shape[0], halo, cu_ref.shape[2]), _F32)

    x = x_ref[...]
    hn = _rms(x, g_ref[layer:layer + 1, :])
    b_gate = _dot(hn, win_ref[:, 0:d])
    cu = _dot(hn, win_ref[:, d:2 * d]) * _dot(hn, win_ref[:, 2 * d:3 * d])
    conv = _causal_conv3(cu_ref, 0, cu, cw_ref[...])
    o_ref[...] = x + _dot(b_gate * conv, wout_ref[...])


def _mixer(x, g, w_in, conv_w, w_out, ffn_w_up, ffn_w_down, layer, a):
    bsz, s, d = x.shape
    tile = ROW_TILE
    grid = (bsz, s // tile)
    cast_in, cast_out, cast_shape = _cast_plan(ffn_w_up, ffn_w_down, layer, grid)
    return pl.pallas_call(
        functools.partial(_mixer_kernel, layer=layer),
        grid=grid,
        in_specs=[_row_spec(tile, d), _const_spec(g.shape), _layer_spec((d, 3 * d), a),
                  _layer_spec((CONV_W, d), a), _layer_spec((d, d), a)] + cast_in,
        out_specs=[_row_spec(tile, d)] + cast_out,
        out_shape=[jax.ShapeDtypeStruct(x.shape, _F32)] + cast_shape,
        scratch_shapes=[pltpu.VMEM((d // V7X_LANES, tile + V7X_SUBLANES, V7X_LANES), _F32)],
        compiler_params=_params(),
        name="sc_mixer",
    )(x, g, w_in, conv_w, w_out, ffn_w_up, ffn_w_down)


def _ffn_kernel(h_ref, g_ref, wup_ref, cw_ref, cb_ref, wdn_ref, fg_ref, o_ref,
                gbuf_ref, act_ref, *, layer, final_norm):
    tile, d = h_ref.shape
    dff = act_ref.shape[1]
    halo = V7X_SUBLANES
    ch = FF_CHUNK

    @pl.when(pl.program_id(1) == 0)
    def _():
        gbuf_ref[:, 0:halo, :] = jnp.zeros((gbuf_ref.shape[0], halo, gbuf_ref.shape[2]), _F32)

    h = h_ref[...]
    hn = _rms(h, g_ref[layer:layer + 1, :]).astype(_BF16)
    for start in range(0, dff, FF_STEP):
        width = min(FF_STEP, dff - start)
        cols = slice(start, start + width)
        g = _dot(hn, wup_ref[:, cols])
        v = _dot(hn, wup_ref[:, dff + start:dff + start + width])
        pre = (_causal_conv3(gbuf_ref, start // ch, g, cw_ref[:, cols])
               + cb_ref[layer:layer + 1, cols])
        act = pre * (1.0 / (1.0 + jnp.exp(-pre))) * v
        act_ref[:, cols] = act.astype(_BF16)
    out = h + _dot(act_ref[...], wdn_ref[...])
    if final_norm:
        out = _rms(out, fg_ref[...])
    o_ref[...] = out


def _ffn(h, g, w_up, conv_w, conv_b, w_down, final_g, *, layer, final_norm):
    bsz, s, d = h.shape
    dff = w_down.shape[0]
    tile = ROW_TILE
    return pl.pallas_call(
        functools.partial(_ffn_kernel, layer=layer, final_norm=final_norm),
        grid=(bsz, s // tile),
        in_specs=[_row_spec(tile, d), _const_spec(g.shape), _const_spec((d, 2 * dff)),
                  _layer_spec((CONV_W, dff), layer), _const_spec(conv_b.shape),
                  _const_spec((dff, d)), _const_spec((1, d))],
        out_specs=_row_spec(tile, d),
        out_shape=jax.ShapeDtypeStruct(h.shape, _F32),
        scratch_shapes=[pltpu.VMEM((dff // FF_CHUNK, tile + V7X_SUBLANES, FF_CHUNK), _F32),
                        pltpu.VMEM((tile, dff), _BF16)],
        compiler_params=_params(),
        name="conv_ffn_final" if final_norm else "conv_ffn",
    )(h, g, w_up, conv_w, conv_b, w_down, final_g)


def _rope_table_t(pos_row, inv_freq_col, scale):
    ang = pos_row.astype(_F32) * inv_freq_col
    cos, sin = jnp.cos(ang), jnp.sin(ang)
    table = jnp.concatenate([cos, cos, -sin, sin], axis=0)
    return table * scale if scale != 1.0 else table


def _kv_kernel(h_ref, pos_ref, freq_ref, gin_ref, wdkv_ref, glat_ref, wkr_ref, wuk_ref,
               wuv_ref, kn_ref, kr_ref, vt_ref):
    hn = _rms(h_ref[...], gin_ref[...])
    c_kv = _rms(_dot(hn, wdkv_ref[...]), glat_ref[...])
    p = _dot(hn, wkr_ref[...]) * _rope_table_t(pos_ref[...], freq_ref[...], 1.0).T
    kr_ref[...] = (p + pltpu.roll(p, QK_ROPE, 1)).astype(_BF16)
    kn_ref[...] = _dot(c_kv, wuk_ref[...]).astype(_BF16)
    vt_ref[...] = _dot(wuv_ref[...].T, c_kv.T).astype(_BF16)


def _shared_kv(h, pos, freq, g_in, w_dkv, g_lat, w_kr2, w_uk, w_uv):
    bsz, s, d = h.shape
    lora = w_dkv.shape[1]
    hk = w_uk.shape[1]
    hv = w_uv.shape[1]
    tile = KV_TILE
    return pl.pallas_call(
        _kv_kernel,
        grid=(bsz, s // tile),
        in_specs=[_row_spec(tile, d), pl.BlockSpec((None, 1, tile), lambda b, i: (b, 0, i)),
                  _const_spec((QK_ROPE // 2, 1)),
                  _const_spec((1, d)), _const_spec((d, lora)), _const_spec((1, lora)),
                  _const_spec((d, 2 * QK_ROPE)), _const_spec((lora, hk)),
                  _const_spec((lora, hv))],
        out_specs=[_row_spec(tile, hk), _row_spec(tile, 2 * QK_ROPE),
                   pl.BlockSpec((None, hv, tile), lambda b, i: (b, 0, i))],
        out_shape=[jax.ShapeDtypeStruct((bsz, s, hk), _BF16),
                   jax.ShapeDtypeStruct((bsz, s, 2 * QK_ROPE), _BF16),
                   jax.ShapeDtypeStruct((bsz, hv, s), _BF16)],
        compiler_params=_params(),
        name="shared_kv",
    )(h, pos, freq, g_in, w_dkv, g_lat, w_kr2, w_uk, w_uv)


def _mla_kernel(h_ref, posr_ref, freqc_ref, g_ref, wdq_ref, gq_ref, wuqt_ref, wo_ref,
                kn_ref, kr_ref, vt_ref, fup_ref, fdn_ref, o_ref, fup_out, fdn_out,
                qt_ref, ob_ref, s_ref, *, layer, a):
    tq, d = h_ref.shape
    i = pl.program_id(1)
    _cast_block(fup_ref, fdn_ref, fup_out, fdn_out)
    scale = float(QK_NOPE + QK_ROPE) ** -0.5 * math.log2(math.e)

    h = h_ref[...]
    hn = _rms(h, g_ref[layer:layer + 1, :])
    c_q = _rms(_dot(hn, wdq_ref[...]), gq_ref[a:a + 1, :])
    c_qt = c_q.T
    table = _rope_table_t(posr_ref[...], freqc_ref[...], scale)
    for hd in range(N_HEADS):
        qt = _dot(wuqt_ref[hd * HEAD_W:(hd + 1) * HEAD_W, :], c_qt)
        qt_ref[hd * HEAD_W:hd * HEAD_W + QK_NOPE, :] = (qt[0:QK_NOPE] * scale).astype(_BF16)
        qt_ref[hd * HEAD_W + QK_NOPE:(hd + 1) * HEAD_W, :] = (qt[QK_NOPE:] * table).astype(_BF16)

    ones_rows = 2 * V7X_SUBLANES
    kb = KEY_BLOCK
    n_sub = tq // kb
    k_chunk = lax.broadcasted_iota(jnp.int32, (kb, kb), 0) // CHUNK
    q_chunk = lax.broadcasted_iota(jnp.int32, (kb, kb), 1) // CHUNK
    diag_mask = k_chunk <= q_chunk
    ones = jnp.ones((ones_rows, kb), _BF16)

    def attend(c):
        n_blocks = (c + 1) * n_sub

        def first_lane(t):
            return max(t - c * n_sub, 0) * kb

        def score_block(hd, t, m8):
            rows = slice(t * kb, (t + 1) * kb)
            lo = first_lane(t)
            k = jnp.concatenate([kn_ref[rows, hd * QK_NOPE:(hd + 1) * QK_NOPE], kr_ref[rows, :]],
                                axis=1)
            s = _dot(k, qt_ref[hd * HEAD_W:(hd + 1) * HEAD_W, lo:tq])
            if t >= c * n_sub:
                s_diag = jnp.where(diag_mask, s[:, 0:kb], NEG_INF)
                s = jnp.concatenate([s_diag, s[:, kb:]], axis=1) if lo + kb < tq else s_diag
            s_ref[hd % 2, rows, lo:tq] = s
            blk_max = jnp.max(s.reshape(kb // V7X_SUBLANES, V7X_SUBLANES, tq - lo), axis=0)
            if m8 is None:
                return blk_max
            if lo == 0:
                return jnp.maximum(m8, blk_max)
            return jnp.concatenate([m8[:, 0:lo], jnp.maximum(m8[:, lo:], blk_max)], axis=1)

        def pv_block(hd, t, m, acc):
            rows = slice(t * kb, (t + 1) * kb)
            lo = first_lane(t)
            p = jnp.exp2(s_ref[hd % 2, rows, lo:tq] - m[:, lo:]).astype(_BF16)
            v_aug = jnp.concatenate([vt_ref[hd * V_HEAD:(hd + 1) * V_HEAD, rows], ones], axis=0)
            o_aug = _dot(v_aug, p)
            if acc is None:
                return o_aug
            if lo == 0:
                return acc + o_aug
            return jnp.concatenate([acc[:, 0:lo], acc[:, lo:] + o_aug], axis=1)

        m8_next = None
        for t in range(n_blocks):
            m8_next = score_block(0, t, m8_next)
        for hd in range(N_HEADS):
            m = jnp.max(m8_next, axis=0, keepdims=True)
            m8_next, acc = None, None
            for t in range(n_blocks):
                if hd + 1 < N_HEADS:
                    m8_next = score_block(hd + 1, t, m8_next)
                acc = pv_block(hd, t, m, acc)
            o_t = acc[0:V_HEAD] * (1.0 / acc[V_HEAD:V_HEAD + 1])
            ob_ref[:, hd * V_HEAD:(hd + 1) * V_HEAD] = o_t.T

    for c in range(kn_ref.shape[0] // tq):
        pl.when(i == c)(functools.partial(attend, c))

    o_ref[...] = h + _dot(ob_ref[...], wo_ref[...])


def _mla(h, pos_row, freq_col, g, w_dq, g_q, w_uq2_t, w_o, k_nope, k_rope, v_t,
         ffn_w_up, ffn_w_down, layer, a):
    bsz, s, d = h.shape
    qlora = w_dq.shape[2]
    hk = k_nope.shape[2]
    hv = v_t.shape[1]
    tq = Q_TILE
    grid = (bsz, s // tq)
    cast_in, cast_out, cast_shape = _cast_plan(ffn_w_up, ffn_w_down, layer, grid)
    return pl.pallas_call(
        functools.partial(_mla_kernel, layer=layer, a=a),
        grid=grid,
        in_specs=[_row_spec(tq, d), pl.BlockSpec((None, 1, tq), lambda b, i: (b, 0, i)),
                  _const_spec((QK_ROPE // 2, 1)),
                  _const_spec(g.shape), _layer_spec((d, qlora), a), _const_spec(g_q.shape),
                  _const_spec((N_HEADS * HEAD_W, qlora)), _layer_spec((hv, d), a),
                  pl.BlockSpec((None, s, hk), lambda b, i: (b, 0, 0)),
                  pl.BlockSpec((None, s, 2 * QK_ROPE), lambda b, i: (b, 0, 0)),
                  pl.BlockSpec((None, hv, s), lambda b, i: (b, 0, 0))] + cast_in,
        out_specs=[_row_spec(tq, d)] + cast_out,
        out_shape=[jax.ShapeDtypeStruct(h.shape, _F32)] + cast_shape,
        scratch_shapes=[pltpu.VMEM((N_HEADS * HEAD_W, tq), _BF16),
                        pltpu.VMEM((tq, hv), _F32),
                        pltpu.VMEM((2, s, tq), _F32)],
        compiler_params=_params(),
        name="mla_attention",
    )(h, pos_row, freq_col, g, w_dq, g_q, w_uq2_t, w_o, k_nope, k_rope, v_t,
      ffn_w_up, ffn_w_down)


def _swap_halves(w):
    half = w.shape[-1] // 2
    return jnp.concatenate([w[..., half:], w[..., :half]], axis=-1)


def kernel(x, positions, attn_norm, ffn_norm, final_norm, sc_w_in, sc_conv_w, sc_w_out, kv_in_norm, w_dkv, kv_latent_norm, w_kr, w_uk, w_uv, w_dq, q_latent_norm, w_uq, w_o, ffn_w_up, ffn_conv_w, ffn_conv_b, ffn_w_down):
    depth = attn_norm.shape[0]
    n_a = sc_w_in.shape[0]
    row = lambda v: v.reshape(1, -1)

    half = QK_ROPE // 2
    inv_freq = 1.0 / (ROPE_THETA ** (jnp.arange(half, dtype=_F32) / half))
    freq_col = inv_freq.reshape(half, 1)
    pos_row = positions.reshape(positions.shape[0], 1, positions.shape[1])

    w_kr2 = jnp.concatenate([w_kr, _swap_halves(w_kr)], axis=1)
    final_g = row(final_norm)

    h = x
    kv = None
    for layer in range(depth):
        if layer < n_a:
            h, w_up, w_down = _mixer(h, attn_norm, sc_w_in, sc_conv_w, sc_w_out,
                                     ffn_w_up, ffn_w_down, layer, layer)
        else:
            a = layer - n_a
            wq = w_uq[a].reshape(-1, N_HEADS, QK_NOPE + QK_ROPE)
            rope_cols = wq[:, :, QK_NOPE:]
            w_uq2 = jnp.concatenate([wq, _swap_halves(rope_cols)], axis=2)
            w_uq2_t = w_uq2.reshape(-1, N_HEADS * HEAD_W).T
            h, w_up, w_down = _mla(h, pos_row, freq_col, attn_norm, w_dq, q_latent_norm,
                                   w_uq2_t, w_o, *kv, ffn_w_up, ffn_w_down, layer, a)
        h = _ffn(h, ffn_norm, w_up, ffn_conv_w, ffn_conv_b, w_down, final_g,
                 layer=layer, final_norm=(layer == depth - 1))
        if layer == n_a - 1:
            kv = _shared_kv(h, pos_row, freq_col, row(kv_in_norm), w_dkv, row(kv_latent_norm),
                            w_kr2, w_uk, w_uv)
    return h
```

```python
import functools
import math

import jax
import jax.numpy as jnp
from jax import lax
from jax.experimental import pallas as pl
from jax.experimental.pallas import tpu as pltpu

CHUNK = 64
CONV_W = 3
N_HEADS = 8
QK_NOPE = 128
QK_ROPE = 64
V_HEAD = 128
ROPE_THETA = 10000.0
EPS = 1e-6
NEG_INF = -1e30

V7X_SUBLANES = 8
V7X_LANES = 128
V7X_MXU_DIM = 256
V7X_VMEM_BYTES = 64 * 1024 * 1024

HEAD_W = QK_NOPE + 2 * QK_ROPE
ROW_TILE = 512
KV_TILE = 1024
Q_TILE = 512
KEY_BLOCK = V7X_MXU_DIM
FF_CHUNK = V7X_MXU_DIM
MIX_CHUNK = V7X_MXU_DIM
VMEM_LIMIT = V7X_VMEM_BYTES - 8 * 1024 * 1024

_BF16 = jnp.bfloat16
_F32 = jnp.float32


def _dot(a, b):
    return jnp.dot(a, b, preferred_element_type=_F32)


def _rms(x, g):
    return x * lax.rsqrt(jnp.mean(x * x, axis=-1, keepdims=True) + EPS) * g


def _const_spec(shape):
    nd = len(shape)
    return pl.BlockSpec(shape, lambda *_: (0,) * nd, pipeline_mode=pl.Buffered(1))


def _layer_spec(shape, layer):
    nd = len(shape)
    return pl.BlockSpec((None,) + tuple(shape), lambda *_: (layer,) + (0,) * nd,
                        pipeline_mode=pl.Buffered(1))


def _row_spec(tile, width):
    return pl.BlockSpec((None, tile, width), lambda b, i: (b, i, 0))


def _cast_plan(w_up, w_down, layer, grid):
    bf16_rows = 2 * V7X_SUBLANES
    n_steps = grid[0] * grid[1]
    _, d, up_cols = w_up.shape
    _, dff, dn_cols = w_down.shape
    up_rows = d // n_steps
    every = next(k for k in range(1, n_steps + 1)
                 if n_steps % k == 0 and (dff * k) % (n_steps * bf16_rows) == 0)
    dn_rows = dff * every // n_steps
    assert up_rows * n_steps == d and up_rows % bf16_rows == 0
    step = lambda b, i: b * grid[1] + i
    in_specs = [pl.BlockSpec((None, up_rows, up_cols), lambda b, i: (layer, step(b, i), 0)),
                pl.BlockSpec((None, dn_rows, dn_cols), lambda b, i: (layer, step(b, i) // every, 0))]
    out_specs = [pl.BlockSpec((up_rows, up_cols), lambda b, i: (step(b, i), 0)),
                 pl.BlockSpec((dn_rows, dn_cols), lambda b, i: (step(b, i) // every, 0))]
    out_shape = [jax.ShapeDtypeStruct((d, up_cols), _BF16),
                 jax.ShapeDtypeStruct((dff, dn_cols), _BF16)]
    return in_specs, out_specs, out_shape


def _cast_block(src_up, src_dn, dst_up, dst_dn):
    dst_up[...] = src_up[...].astype(_BF16)
    dst_dn[...] = src_dn[...].astype(_BF16)


def _params():
    return pltpu.CompilerParams(
        dimension_semantics=("arbitrary", "arbitrary"), vmem_limit_bytes=VMEM_LIMIT)


def _causal_conv3(slab_ref, first, cur, w):
    halo, lanes = V7X_SUBLANES, slab_ref.shape[2]
    tile = cur.shape[0]
    outs = []
    for j in range(cur.shape[1] // lanes):
        cols = slice(j * lanes, (j + 1) * lanes)
        slab = slab_ref.at[first + j]
        slab[halo:halo + tile, :] = cur[:, cols]
        y = slab[pl.ds(halo - 2, tile), :] * w[0:1, cols]
        y = y + slab[pl.ds(halo - 1, tile), :] * w[1:2, cols]
        outs.append(y + cur[:, cols] * w[2:3, cols])
        slab[0:halo, :] = slab[tile:tile + halo, :]
    return jnp.concatenate(outs, axis=1)


def _mixer_kernel(x_ref, g_ref, win_ref, cw_ref, wout_ref, fup_ref, fdn_ref,
                  o_ref, fup_out, fdn_out, cu_ref, y_ref, *, layer):
    tile, d = x_ref.shape
    halo = V7X_SUBLANES
    _cast_block(fup_ref, fdn_ref, fup_out, fdn_out)

    @pl.when(pl.program_id(1) == 0)
    def _():
        cu_ref[:, 0:halo, :] = jnp.zeros((cu_ref.shape[0], halo, cu_ref.shape[2]), _F32)

    x = x_ref[...]
    hn = _rms(x, g_ref[layer:layer + 1, :])
    ch = MIX_CHUNK
    for c in range(d // ch):
        cols = slice(c * ch, (c + 1) * ch)
        b_gate = _dot(hn, win_ref[:, cols])
        cu = (_dot(hn, win_ref[:, d + c * ch:d + (c + 1) * ch])
              * _dot(hn, win_ref[:, 2 * d + c * ch:2 * d + (c + 1) * ch]))
        conv = _causal_conv3(cu_ref, c * (ch // cu_ref.shape[2]), cu, cw_ref[:, cols])
        y_ref[:, cols] = b_gate * conv
    o_ref[...] = x + _dot(y_ref[...], wout_ref[...])


def _mixer(x, g, w_in, conv_w, w_out, ffn_w_up, ffn_w_down, layer, a):
    bsz, s, d = x.shape
    tile = ROW_TILE
    grid = (bsz, s // tile)
    cast_in, cast_out, cast_shape = _cast_plan(ffn_w_up, ffn_w_down, layer, grid)
    return pl.pallas_call(
        functools.partial(_mixer_kernel, layer=layer),
        grid=grid,
        in_specs=[_row_spec(tile, d), _const_spec(g.shape), _layer_spec((d, 3 * d), a),
                  _layer_spec((CONV_W, d), a), _layer_spec((d, d), a)] + cast_in,
        out_specs=[_row_spec(tile, d)] + cast_out,
        out_shape=[jax.ShapeDtypeStruct(x.shape, _F32)] + cast_shape,
        scratch_shapes=[pltpu.VMEM((d // V7X_LANES, tile + V7X_SUBLANES, V7X_LANES), _F32),
                        pltpu.VMEM((tile, d), _F32)],
        compiler_params=_params(),
        name="sc_mixer",
    )(x, g, w_in, conv_w, w_out, ffn_w_up, ffn_w_down)


def _ffn_kernel(h_ref, g_ref, wup_ref, cw_ref, cb_ref, wdn_ref, fg_ref, o_ref,
                gbuf_ref, act_ref, *, layer, final_norm):
    tile, d = h_ref.shape
    dff = act_ref.shape[1]
    halo = V7X_SUBLANES
    ch = FF_CHUNK

    @pl.when(pl.program_id(1) == 0)
    def _():
        gbuf_ref[:, 0:halo, :] = jnp.zeros((gbuf_ref.shape[0], halo, gbuf_ref.shape[2]), _F32)

    h = h_ref[...]
    hn = _rms(h, g_ref[layer:layer + 1, :]).astype(_BF16)
    for c in range(dff // ch):
        cols = slice(c * ch, (c + 1) * ch)
        g = _dot(hn, wup_ref[:, cols])
        v = _dot(hn, wup_ref[:, dff + c * ch:dff + (c + 1) * ch])
        pre = _causal_conv3(gbuf_ref, c, g, cw_ref[:, cols]) + cb_ref[layer:layer + 1, cols]
        act = pre * (1.0 / (1.0 + jnp.exp(-pre))) * v
        act_ref[:, cols] = act.astype(_BF16)
    out = h + _dot(act_ref[...], wdn_ref[...])
    if final_norm:
        out = _rms(out, fg_ref[...])
    o_ref[...] = out


def _ffn(h, g, w_up, conv_w, conv_b, w_down, final_g, *, layer, final_norm):
    bsz, s, d = h.shape
    dff = w_down.shape[0]
    tile = ROW_TILE
    return pl.pallas_call(
        functools.partial(_ffn_kernel, layer=layer, final_norm=final_norm),
        grid=(bsz, s // tile),
        in_specs=[_row_spec(tile, d), _const_spec(g.shape), _const_spec((d, 2 * dff)),
                  _layer_spec((CONV_W, dff), layer), _const_spec(conv_b.shape),
                  _const_spec((dff, d)), _const_spec((1, d))],
        out_specs=_row_spec(tile, d),
        out_shape=jax.ShapeDtypeStruct(h.shape, _F32),
        scratch_shapes=[pltpu.VMEM((dff // FF_CHUNK, tile + V7X_SUBLANES, FF_CHUNK), _F32),
                        pltpu.VMEM((tile, dff), _BF16)],
        compiler_params=_params(),
        name="conv_ffn_final" if final_norm else "conv_ffn",
    )(h, g, w_up, conv_w, conv_b, w_down, final_g)


def _rope_table_t(pos_row, inv_freq_col, scale):
    ang = pos_row.astype(_F32) * inv_freq_col
    cos, sin = jnp.cos(ang), jnp.sin(ang)
    table = jnp.concatenate([cos, cos, -sin, sin], axis=0)
    return table * scale if scale != 1.0 else table


def _kv_kernel(h_ref, pos_ref, freq_ref, gin_ref, wdkv_ref, glat_ref, wkr_ref, wuk_ref,
               wuv_ref, kn_ref, kr_ref, vt_ref):
    hn = _rms(h_ref[...], gin_ref[...])
    c_kv = _rms(_dot(hn, wdkv_ref[...]), glat_ref[...])
    p = _dot(hn, wkr_ref[...]) * _rope_table_t(pos_ref[...], freq_ref[...], 1.0).T
    kr_ref[...] = (p + pltpu.roll(p, QK_ROPE, 1)).astype(_BF16)
    kn_ref[...] = _dot(c_kv, wuk_ref[...]).astype(_BF16)
    vt_ref[...] = _dot(wuv_ref[...].T, c_kv.T).astype(_BF16)


def _shared_kv(h, pos, freq, g_in, w_dkv, g_lat, w_kr2, w_uk, w_uv):
    bsz, s, d = h.shape
    lora = w_dkv.shape[1]
    hk = w_uk.shape[1]
    hv = w_uv.shape[1]
    tile = KV_TILE
    return pl.pallas_call(
        _kv_kernel,
        grid=(bsz, s // tile),
        in_specs=[_row_spec(tile, d), pl.BlockSpec((None, 1, tile), lambda b, i: (b, 0, i)),
                  _const_spec((QK_ROPE // 2, 1)),
                  _const_spec((1, d)), _const_spec((d, lora)), _const_spec((1, lora)),
                  _const_spec((d, 2 * QK_ROPE)), _const_spec((lora, hk)),
                  _const_spec((lora, hv))],
        out_specs=[_row_spec(tile, hk), _row_spec(tile, 2 * QK_ROPE),
                   pl.BlockSpec((None, hv, tile), lambda b, i: (b, 0, i))],
        out_shape=[jax.ShapeDtypeStruct((bsz, s, hk), _BF16),
                   jax.ShapeDtypeStruct((bsz, s, 2 * QK_ROPE), _BF16),
                   jax.ShapeDtypeStruct((bsz, hv, s), _BF16)],
        compiler_params=_params(),
        name="shared_kv",
    )(h, pos, freq, g_in, w_dkv, g_lat, w_kr2, w_uk, w_uv)


def _mla_kernel(h_ref, posr_ref, freqc_ref, g_ref, wdq_ref, gq_ref, wuqt_ref, wo_ref,
                kn_ref, kr_ref, vt_ref, fup_ref, fdn_ref, o_ref, fup_out, fdn_out,
                qt_ref, ob_ref, s_ref, *, layer, a):
    tq, d = h_ref.shape
    i = pl.program_id(1)
    _cast_block(fup_ref, fdn_ref, fup_out, fdn_out)
    scale = float(QK_NOPE + QK_ROPE) ** -0.5 * math.log2(math.e)

    h = h_ref[...]
    hn = _rms(h, g_ref[layer:layer + 1, :])
    c_q = _rms(_dot(hn, wdq_ref[...]), gq_ref[a:a + 1, :])
    c_qt = c_q.T
    table = _rope_table_t(posr_ref[...], freqc_ref[...], scale)
    for hd in range(N_HEADS):
        qt = _dot(wuqt_ref[hd * HEAD_W:(hd + 1) * HEAD_W, :], c_qt)
        qt_ref[hd * HEAD_W:hd * HEAD_W + QK_NOPE, :] = (qt[0:QK_NOPE] * scale).astype(_BF16)
        qt_ref[hd * HEAD_W + QK_NOPE:(hd + 1) * HEAD_W, :] = (qt[QK_NOPE:] * table).astype(_BF16)

    ones_rows = 2 * V7X_SUBLANES
    kb = KEY_BLOCK
    n_sub = tq // kb
    k_chunk = lax.broadcasted_iota(jnp.int32, (kb, kb), 0) // CHUNK
    q_chunk = lax.broadcasted_iota(jnp.int32, (kb, kb), 1) // CHUNK
    diag_mask = k_chunk <= q_chunk
    ones = jnp.ones((ones_rows, kb), _BF16)

    def attend(c):
        n_blocks = (c + 1) * n_sub

        def first_lane(t):
            return max(t - c * n_sub, 0) * kb

        def score_block(hd, t, m8):
            rows = slice(t * kb, (t + 1) * kb)
            lo = first_lane(t)
            k = jnp.concatenate([kn_ref[rows, hd * QK_NOPE:(hd + 1) * QK_NOPE], kr_ref[rows, :]],
                                axis=1)
            s = _dot(k, qt_ref[hd * HEAD_W:(hd + 1) * HEAD_W, lo:tq])
            if t >= c * n_sub:
                s_diag = jnp.where(diag_mask, s[:, 0:kb], NEG_INF)
                s = jnp.concatenate([s_diag, s[:, kb:]], axis=1) if lo + kb < tq else s_diag
            s_ref[hd % 2, rows, lo:tq] = s
            blk_max = jnp.max(s.reshape(kb // V7X_SUBLANES, V7X_SUBLANES, tq - lo), axis=0)
            if m8 is None:
                return blk_max
            if lo == 0:
                return jnp.maximum(m8, blk_max)
            return jnp.concatenate([m8[:, 0:lo], jnp.maximum(m8[:, lo:], blk_max)], axis=1)

        def pv_block(hd, t, m, acc):
            rows = slice(t * kb, (t + 1) * kb)
            lo = first_lane(t)
            p = jnp.exp2(s_ref[hd % 2, rows, lo:tq] - m[:, lo:]).astype(_BF16)
            v_aug = jnp.concatenate([vt_ref[hd * V_HEAD:(hd + 1) * V_HEAD, rows], ones], axis=0)
            o_aug = _dot(v_aug, p)
            if acc is None:
                return o_aug
            if lo == 0:
                return acc + o_aug
            return jnp.concatenate([acc[:, 0:lo], acc[:, lo:] + o_aug], axis=1)

        m8_next = None
        for t in range(n_blocks):
            m8_next = score_block(0, t, m8_next)
        for hd in range(N_HEADS):
            m = jnp.max(m8_next, axis=0, keepdims=True)
            m8_next, acc = None, None
            for t in range(n_blocks):
                if hd + 1 < N_HEADS:
                    m8_next = score_block(hd + 1, t, m8_next)
                acc = pv_block(hd, t, m, acc)
            o_t = acc[0:V_HEAD] * (1.0 / acc[V_HEAD:V_HEAD + 1])
            ob_ref[:, hd * V_HEAD:(hd + 1) * V_HEAD] = o_t.T

    for c in range(kn_ref.shape[0] // tq):
        pl.when(i == c)(functools.partial(attend, c))

    o_ref[...] = h + _dot(ob_ref[...], wo_ref[...])


def _mla(h, pos_row, freq_col, g, w_dq, g_q, w_uq2_t, w_o, k_nope, k_rope, v_t,
         ffn_w_up, ffn_w_down, layer, a):
    bsz, s, d = h.shape
    qlora = w_dq.shape[2]
    hk = k_nope.shape[2]
    hv = v_t.shape[1]
    tq = Q_TILE
    grid = (bsz, s // tq)
    cast_in, cast_out, cast_shape = _cast_plan(ffn_w_up, ffn_w_down, layer, grid)
    return pl.pallas_call(
        functools.partial(_mla_kernel, layer=layer, a=a),
        grid=grid,
        in_specs=[_row_spec(tq, d), pl.BlockSpec((None, 1, tq), lambda b, i: (b, 0, i)),
                  _const_spec((QK_ROPE // 2, 1)),
                  _const_spec(g.shape), _layer_spec((d, qlora), a), _const_spec(g_q.shape),
                  _const_spec((N_HEADS * HEAD_W, qlora)), _layer_spec((hv, d), a),
                  pl.BlockSpec((None, s, hk), lambda b, i: (b, 0, 0)),
                  pl.BlockSpec((None, s, 2 * QK_ROPE), lambda b, i: (b, 0, 0)),
                  pl.BlockSpec((None, hv, s), lambda b, i: (b, 0, 0))] + cast_in,
        out_specs=[_row_spec(tq, d)] + cast_out,
        out_shape=[jax.ShapeDtypeStruct(h.shape, _F32)] + cast_shape,
        scratch_shapes=[pltpu.VMEM((N_HEADS * HEAD_W, tq), _BF16),
                        pltpu.VMEM((tq, hv), _F32),
                        pltpu.VMEM((2, s, tq), _F32)],
        compiler_params=_params(),
        name="mla_attention",
    )(h, pos_row, freq_col, g, w_dq, g_q, w_uq2_t, w_o, k_nope, k_rope, v_t,
      ffn_w_up, ffn_w_down)


def _swap_halves(w):
    half = w.shape[-1] // 2
    return jnp.concatenate([w[..., half:], w[..., :half]], axis=-1)


def kernel(x, positions, attn_norm, ffn_norm, final_norm, sc_w_in, sc_conv_w, sc_w_out, kv_in_norm, w_dkv, kv_latent_norm, w_kr, w_uk, w_uv, w_dq, q_latent_norm, w_uq, w_o, ffn_w_up, ffn_conv_w, ffn_conv_b, ffn_w_down):
    depth = attn_norm.shape[0]
    n_a = sc_w_in.shape[0]
    row = lambda v: v.reshape(1, -1)

    half = QK_ROPE // 2
    inv_freq = 1.0 / (ROPE_THETA ** (jnp.arange(half, dtype=_F32) / half))
    freq_col = inv_freq.reshape(half, 1)
    pos_row = positions.reshape(positions.shape[0], 1, positions.shape[1])

    w_kr2 = jnp.concatenate([w_kr, _swap_halves(w_kr)], axis=1)
    final_g = row(final_norm)

    h = x
    kv = None
    for layer in range(depth):
        if layer < n_a:
            h, w_up, w_down = _mixer(h, attn_norm, sc_w_in, sc_conv_w, sc_w_out,
                                     ffn_w_up, ffn_w_down, layer, layer)
        else:
            a = layer - n_a
            wq = w_uq[a].reshape(-1, N_HEADS, QK_NOPE + QK_ROPE)
            rope_cols = wq[:, :, QK_NOPE:]
            w_uq2 = jnp.concatenate([wq, _swap_halves(rope_cols)], axis=2)
            w_uq2_t = w_uq2.reshape(-1, N_HEADS * HEAD_W).T
            h, w_up, w_down = _mla(h, pos_row, freq_col, attn_norm, w_dq, q_latent_norm,
                                   w_uq2_t, w_o, *kv, ffn_w_up, ffn_w_down, layer, a)
        h = _ffn(h, ffn_norm, w_up, ffn_conv_w, ffn_conv_b, w_down, final_g,
                 layer=layer, final_norm=(layer == depth - 1))
        if layer == n_a - 1:
            kv = _shared_kv(h, pos_row, freq_col, row(kv_in_norm), w_dkv, row(kv_latent_norm),
                            w_kr2, w_uk, w_uv)
    return h
```

```python
import functools
import math

import jax
import jax.numpy as jnp
from jax import lax
from jax.experimental import pallas as pl
from jax.experimental.pallas import tpu as pltpu

CHUNK = 64
CONV_W = 3
N_HEADS = 8
QK_NOPE = 128
QK_ROPE = 64
V_HEAD = 128
ROPE_THETA = 10000.0
EPS = 1e-6
NEG_INF = -1e30

V7X_SUBLANES = 8
V7X_LANES = 128
V7X_MXU_DIM = 256
V7X_VMEM_BYTES = 64 * 1024 * 1024

HEAD_W = QK_NOPE + 2 * QK_ROPE
ROW_TILE = 512
KV_TILE = 1024
Q_TILE = 512
KEY_BLOCK = V7X_MXU_DIM
FF_CHUNK = V7X_MXU_DIM
VMEM_LIMIT = V7X_VMEM_BYTES - 8 * 1024 * 1024

_BF16 = jnp.bfloat16
_F32 = jnp.float32


def _dot(a, b):
    return jnp.dot(a, b, preferred_element_type=_F32)


def _rms(x, g):
    return x * lax.rsqrt(jnp.mean(x * x, axis=-1, keepdims=True) + EPS) * g


def _const_spec(shape):
    nd = len(shape)
    return pl.BlockSpec(shape, lambda *_: (0,) * nd, pipeline_mode=pl.Buffered(1))


def _layer_spec(shape, layer):
    nd = len(shape)
    return pl.BlockSpec((None,) + tuple(shape), lambda *_: (layer,) + (0,) * nd,
                        pipeline_mode=pl.Buffered(1))


def _row_spec(tile, width):
    return pl.BlockSpec((None, tile, width), lambda b, i: (b, i, 0))


def _cast_plan(w_up, w_down, layer, grid):
    bf16_rows = 2 * V7X_SUBLANES
    n_steps = grid[0] * grid[1]
    _, d, up_cols = w_up.shape
    _, dff, dn_cols = w_down.shape
    up_rows = d // n_steps
    every = next(k for k in range(1, n_steps + 1)
                 if n_steps % k == 0 and (dff * k) % (n_steps * bf16_rows) == 0)
    dn_rows = dff * every // n_steps
    assert up_rows * n_steps == d and up_rows % bf16_rows == 0
    step = lambda b, i: b * grid[1] + i
    in_specs = [pl.BlockSpec((None, up_rows, up_cols), lambda b, i: (layer, step(b, i), 0)),
                pl.BlockSpec((None, dn_rows, dn_cols), lambda b, i: (layer, step(b, i) // every, 0))]
    out_specs = [pl.BlockSpec((up_rows, up_cols), lambda b, i: (step(b, i), 0)),
                 pl.BlockSpec((dn_rows, dn_cols), lambda b, i: (step(b, i) // every, 0))]
    out_shape = [jax.ShapeDtypeStruct((d, up_cols), _BF16),
                 jax.ShapeDtypeStruct((dff, dn_cols), _BF16)]
    return in_specs, out_specs, out_shape


def _cast_block(src_up, src_dn, dst_up, dst_dn):
    dst_up[...] = src_up[...].astype(_BF16)
    dst_dn[...] = src_dn[...].astype(_BF16)


def _params():
    return pltpu.CompilerParams(
        dimension_semantics=("arbitrary", "arbitrary"), vmem_limit_bytes=VMEM_LIMIT)


def _causal_conv3(slab_ref, first, cur, w):
    halo, lanes = V7X_SUBLANES, slab_ref.shape[2]
    tile = cur.shape[0]
    outs = []
    for j in range(cur.shape[1] // lanes):
        cols = slice(j * lanes, (j + 1) * lanes)
        slab = slab_ref.at[first + j]
        slab[halo:halo + tile, :] = cur[:, cols]
        y = slab[pl.ds(halo - 2, tile), :] * w[0:1, cols]
        y = y + slab[pl.ds(halo - 1, tile), :] * w[1:2, cols]
        outs.append(y + cur[:, cols] * w[2:3, cols])
        slab[0:halo, :] = slab[tile:tile + halo, :]
    return jnp.concatenate(outs, axis=1)


def _causal_conv3_shifted(slab_ref, first, cur, w):
    halo, lanes = V7X_SUBLANES, slab_ref.shape[3]
    tile = cur.shape[0]
    outs = []
    for j in range(cur.shape[1] // lanes):
        cols = slice(j * lanes, (j + 1) * lanes)
        x = cur[:, cols]
        one, two = slab_ref.at[first + j, 0], slab_ref.at[first + j, 1]
        one[pl.ds(halo + 1, tile), :] = x
        two[pl.ds(halo + 2, tile), :] = x
        y = two[halo:halo + tile, :] * w[0:1, cols] + one[halo:halo + tile, :] * w[1:2, cols]
        outs.append(y + x * w[2:3, cols])
        one[halo:halo + 1, :] = one[halo + tile:halo + tile + 1, :]
        two[halo:halo + 2, :] = two[halo + tile:halo + tile + 2, :]
    return jnp.concatenate(outs, axis=1)


def _mixer_kernel(x_ref, g_ref, win_ref, cw_ref, wout_ref, fup_ref, fdn_ref,
                  o_ref, fup_out, fdn_out, cu_ref, *, layer):
    tile, d = x_ref.shape
    halo = V7X_SUBLANES
    _cast_block(fup_ref, fdn_ref, fup_out, fdn_out)

    @pl.when(pl.program_id(1) == 0)
    def _():
        cu_ref[:, 0:halo, :] = jnp.zeros((cu_ref.shape[0], halo, cu_ref.shape[2]), _F32)

    x = x_ref[...]
    hn = _rms(x, g_ref[layer:layer + 1, :])
    b_gate = _dot(hn, win_ref[:, 0:d])
    cu = _dot(hn, win_ref[:, d:2 * d]) * _dot(hn, win_ref[:, 2 * d:3 * d])
    conv = _causal_conv3(cu_ref, 0, cu, cw_ref[...])
    o_ref[...] = x + _dot(b_gate * conv, wout_ref[...])


def _mixer(x, g, w_in, conv_w, w_out, ffn_w_up, ffn_w_down, layer, a):
    bsz, s, d = x.shape
    tile = ROW_TILE
    grid = (bsz, s // tile)
    cast_in, cast_out, cast_shape = _cast_plan(ffn_w_up, ffn_w_down, layer, grid)
    return pl.pallas_call(
        functools.partial(_mixer_kernel, layer=layer),
        grid=grid,
        in_specs=[_row_spec(tile, d), _const_spec(g.shape), _layer_spec((d, 3 * d), a),
                  _layer_spec((CONV_W, d), a), _layer_spec((d, d), a)] + cast_in,
        out_specs=[_row_spec(tile, d)] + cast_out,
        out_shape=[jax.ShapeDtypeStruct(x.shape, _F32)] + cast_shape,
        scratch_shapes=[pltpu.VMEM((d // V7X_LANES, tile + V7X_SUBLANES, V7X_LANES), _F32)],
        compiler_params=_params(),
        name="sc_mixer",
    )(x, g, w_in, conv_w, w_out, ffn_w_up, ffn_w_down)


def _ffn_kernel(h_ref, g_ref, wup_ref, cw_ref, cb_ref, wdn_ref, fg_ref, o_ref,
                gbuf_ref, act_ref, *, layer, final_norm):
    tile, d = h_ref.shape
    dff = act_ref.shape[1]
    halo = V7X_SUBLANES
    ch = FF_CHUNK

    @pl.when(pl.program_id(1) == 0)
    def _():
        gbuf_ref[:, :, 0:2 * halo, :] = jnp.zeros(gbuf_ref.shape[0:2] + (2 * halo, gbuf_ref.shape[3]),
                                                  _F32)

    h = h_ref[...]
    hn = _rms(h, g_ref[layer:layer + 1, :]).astype(_BF16)
    for c in range(dff // ch):
        cols = slice(c * ch, (c + 1) * ch)
        g = _dot(hn, wup_ref[:, cols])
        v = _dot(hn, wup_ref[:, dff + c * ch:dff + (c + 1) * ch])
        pre = (_causal_conv3_shifted(gbuf_ref, c * (ch // V7X_LANES), g, cw_ref[:, cols])
               + cb_ref[layer:layer + 1, cols])
        act = pre * (1.0 / (1.0 + jnp.exp(-pre))) * v
        act_ref[:, cols] = act.astype(_BF16)
    out = h + _dot(act_ref[...], wdn_ref[...])
    if final_norm:
        out = _rms(out, fg_ref[...])
    o_ref[...] = out


def _ffn(h, g, w_up, conv_w, conv_b, w_down, final_g, *, layer, final_norm):
    bsz, s, d = h.shape
    dff = w_down.shape[0]
    tile = ROW_TILE
    return pl.pallas_call(
        functools.partial(_ffn_kernel, layer=layer, final_norm=final_norm),
        grid=(bsz, s // tile),
        in_specs=[_row_spec(tile, d), _const_spec(g.shape), _const_spec((d, 2 * dff)),
                  _layer_spec((CONV_W, dff), layer), _const_spec(conv_b.shape),
                  _const_spec((dff, d)), _const_spec((1, d))],
        out_specs=_row_spec(tile, d),
        out_shape=jax.ShapeDtypeStruct(h.shape, _F32),
        scratch_shapes=[pltpu.VMEM((dff // V7X_LANES, 2, tile + 2 * V7X_SUBLANES, V7X_LANES), _F32),
                        pltpu.VMEM((tile, dff), _BF16)],
        compiler_params=_params(),
        name="conv_ffn_final" if final_norm else "conv_ffn",
    )(h, g, w_up, conv_w, conv_b, w_down, final_g)


def _rope_table_t(pos_row, inv_freq_col, scale):
    ang = pos_row.astype(_F32) * inv_freq_col
    cos, sin = jnp.cos(ang), jnp.sin(ang)
    table = jnp.concatenate([cos, cos, -sin, sin], axis=0)
    return table * scale if scale != 1.0 else table


def _kv_kernel(h_ref, pos_ref, freq_ref, gin_ref, wdkv_ref, glat_ref, wkr_ref, wuk_ref,
               wuv_ref, kn_ref, kr_ref, vt_ref):
    hn = _rms(h_ref[...], gin_ref[...])
    c_kv = _rms(_dot(hn, wdkv_ref[...]), glat_ref[...])
    p = _dot(hn, wkr_ref[...]) * _rope_table_t(pos_ref[...], freq_ref[...], 1.0).T
    kr_ref[...] = (p + pltpu.roll(p, QK_ROPE, 1)).astype(_BF16)
    kn_ref[...] = _dot(c_kv, wuk_ref[...]).astype(_BF16)
    vt_ref[...] = _dot(wuv_ref[...].T, c_kv.T).astype(_BF16)


def _shared_kv(h, pos, freq, g_in, w_dkv, g_lat, w_kr2, w_uk, w_uv):
    bsz, s, d = h.shape
    lora = w_dkv.shape[1]
    hk = w_uk.shape[1]
    hv = w_uv.shape[1]
    tile = KV_TILE
    return pl.pallas_call(
        _kv_kernel,
        grid=(bsz, s // tile),
        in_specs=[_row_spec(tile, d), pl.BlockSpec((None, 1, tile), lambda b, i: (b, 0, i)),
                  _const_spec((QK_ROPE // 2, 1)),
                  _const_spec((1, d)), _const_spec((d, lora)), _const_spec((1, lora)),
                  _const_spec((d, 2 * QK_ROPE)), _const_spec((lora, hk)),
                  _const_spec((lora, hv))],
        out_specs=[_row_spec(tile, hk), _row_spec(tile, 2 * QK_ROPE),
                   pl.BlockSpec((None, hv, tile), lambda b, i: (b, 0, i))],
        out_shape=[jax.ShapeDtypeStruct((bsz, s, hk), _BF16),
                   jax.ShapeDtypeStruct((bsz, s, 2 * QK_ROPE), _BF16),
                   jax.ShapeDtypeStruct((bsz, hv, s), _BF16)],
        compiler_params=_params(),
        name="shared_kv",
    )(h, pos, freq, g_in, w_dkv, g_lat, w_kr2, w_uk, w_uv)


def _mla_kernel(h_ref, posr_ref, freqc_ref, g_ref, wdq_ref, gq_ref, wuqt_ref, wo_ref,
                kn_ref, kr_ref, vt_ref, fup_ref, fdn_ref, o_ref, fup_out, fdn_out,
                qt_ref, ob_ref, s_ref, *, layer, a):
    tq, d = h_ref.shape
    i = pl.program_id(1)
    _cast_block(fup_ref, fdn_ref, fup_out, fdn_out)
    scale = float(QK_NOPE + QK_ROPE) ** -0.5 * math.log2(math.e)

    h = h_ref[...]
    hn = _rms(h, g_ref[layer:layer + 1, :])
    c_q = _rms(_dot(hn, wdq_ref[...]), gq_ref[a:a + 1, :])
    c_qt = c_q.T
    table = _rope_table_t(posr_ref[...], freqc_ref[...], scale)
    for hd in range(N_HEADS):
        qt = _dot(wuqt_ref[hd * HEAD_W:(hd + 1) * HEAD_W, :], c_qt)
        qt_ref[hd * HEAD_W:hd * HEAD_W + QK_NOPE, :] = (qt[0:QK_NOPE] * scale).astype(_BF16)
        qt_ref[hd * HEAD_W + QK_NOPE:(hd + 1) * HEAD_W, :] = (qt[QK_NOPE:] * table).astype(_BF16)

    ones_rows = 2 * V7X_SUBLANES
    kb = KEY_BLOCK
    n_sub = tq // kb
    k_chunk = lax.broadcasted_iota(jnp.int32, (kb, kb), 0) // CHUNK
    q_chunk = lax.broadcasted_iota(jnp.int32, (kb, kb), 1) // CHUNK
    diag_mask = k_chunk <= q_chunk
    ones = jnp.ones((ones_rows, kb), _BF16)

    def attend(c):
        n_blocks = (c + 1) * n_sub

        def first_lane(t):
            return max(t - c * n_sub, 0) * kb

        def score_block(hd, t, m8):
            rows = slice(t * kb, (t + 1) * kb)
            lo = first_lane(t)
            k = jnp.concatenate([kn_ref[rows, hd * QK_NOPE:(hd + 1) * QK_NOPE], kr_ref[rows, :]],
                                axis=1)
            s = _dot(k, qt_ref[hd * HEAD_W:(hd + 1) * HEAD_W, lo:tq])
            if t >= c * n_sub:
                s_diag = jnp.where(diag_mask, s[:, 0:kb], NEG_INF)
                s = jnp.concatenate([s_diag, s[:, kb:]], axis=1) if lo + kb < tq else s_diag
            s_ref[hd % 2, rows, lo:tq] = s
            blk_max = jnp.max(s.reshape(kb // V7X_SUBLANES, V7X_SUBLANES, tq - lo), axis=0)
            if m8 is None:
                return blk_max
            if lo == 0:
                return jnp.maximum(m8, blk_max)
            return jnp.concatenate([m8[:, 0:lo], jnp.maximum(m8[:, lo:], blk_max)], axis=1)

        def pv_block(hd, t, m, acc):
            rows = slice(t * kb, (t + 1) * kb)
            lo = first_lane(t)
            p = jnp.exp2(s_ref[hd % 2, rows, lo:tq] - m[:, lo:]).astype(_BF16)
            v_aug = jnp.concatenate([vt_ref[hd * V_HEAD:(hd + 1) * V_HEAD, rows], ones], axis=0)
            o_aug = _dot(v_aug, p)
            if acc is None:
                return o_aug
            if lo == 0:
                return acc + o_aug
            return jnp.concatenate([acc[:, 0:lo], acc[:, lo:] + o_aug], axis=1)

        m8_next = None
        for t in range(n_blocks):
            m8_next = score_block(0, t, m8_next)
        for hd in range(N_HEADS):
            m = jnp.max(m8_next, axis=0, keepdims=True)
            m8_next, acc = None, None
            for t in range(n_blocks):
                if hd + 1 < N_HEADS:
                    m8_next = score_block(hd + 1, t, m8_next)
                acc = pv_block(hd, t, m, acc)
            o_t = acc[0:V_HEAD] * (1.0 / acc[V_HEAD:V_HEAD + 1])
            ob_ref[:, hd * V_HEAD:(hd + 1) * V_HEAD] = o_t.T

    for c in range(kn_ref.shape[0] // tq):
        pl.when(i == c)(functools.partial(attend, c))

    o_ref[...] = h + _dot(ob_ref[...], wo_ref[...])


def _mla(h, pos_row, freq_col, g, w_dq, g_q, w_uq2_t, w_o, k_nope, k_rope, v_t,
         ffn_w_up, ffn_w_down, layer, a):
    bsz, s, d = h.shape
    qlora = w_dq.shape[2]
    hk = k_nope.shape[2]
    hv = v_t.shape[1]
    tq = Q_TILE
    grid = (bsz, s // tq)
    cast_in, cast_out, cast_shape = _cast_plan(ffn_w_up, ffn_w_down, layer, grid)
    return pl.pallas_call(
        functools.partial(_mla_kernel, layer=layer, a=a),
        grid=grid,
        in_specs=[_row_spec(tq, d), pl.BlockSpec((None, 1, tq), lambda b, i: (b, 0, i)),
                  _const_spec((QK_ROPE // 2, 1)),
                  _const_spec(g.shape), _layer_spec((d, qlora), a), _const_spec(g_q.shape),
                  _const_spec((N_HEADS * HEAD_W, qlora)), _layer_spec((hv, d), a),
                  pl.BlockSpec((None, s, hk), lambda b, i: (b, 0, 0)),
                  pl.BlockSpec((None, s, 2 * QK_ROPE), lambda b, i: (b, 0, 0)),
                  pl.BlockSpec((None, hv, s), lambda b, i: (b, 0, 0))] + cast_in,
        out_specs=[_row_spec(tq, d)] + cast_out,
        out_shape=[jax.ShapeDtypeStruct(h.shape, _F32)] + cast_shape,
        scratch_shapes=[pltpu.VMEM((N_HEADS * HEAD_W, tq), _BF16),
                        pltpu.VMEM((tq, hv), _F32),
                        pltpu.VMEM((2, s, tq), _F32)],
        compiler_params=_params(),
        name="mla_attention",
    )(h, pos_row, freq_col, g, w_dq, g_q, w_uq2_t, w_o, k_nope, k_rope, v_t,
      ffn_w_up, ffn_w_down)


def _swap_halves(w):
    half = w.shape[-1] // 2
    return jnp.concatenate([w[..., half:], w[..., :half]], axis=-1)


def kernel(x, positions, attn_norm, ffn_norm, final_norm, sc_w_in, sc_conv_w, sc_w_out, kv_in_norm, w_dkv, kv_latent_norm, w_kr, w_uk, w_uv, w_dq, q_latent_norm, w_uq, w_o, ffn_w_up, ffn_conv_w, ffn_conv_b, ffn_w_down):
    depth = attn_norm.shape[0]
    n_a = sc_w_in.shape[0]
    row = lambda v: v.reshape(1, -1)

    half = QK_ROPE // 2
    inv_freq = 1.0 / (ROPE_THETA ** (jnp.arange(half, dtype=_F32) / half))
    freq_col = inv_freq.reshape(half, 1)
    pos_row = positions.reshape(positions.shape[0], 1, positions.shape[1])

    w_kr2 = jnp.concatenate([w_kr, _swap_halves(w_kr)], axis=1)
    final_g = row(final_norm)

    h = x
    kv = None
    for layer in range(depth):
        if layer < n_a:
            h, w_up, w_down = _mixer(h, attn_norm, sc_w_in, sc_conv_w, sc_w_out,
                                     ffn_w_up, ffn_w_down, layer, layer)
        else:
            a = layer - n_a
            wq = w_uq[a].reshape(-1, N_HEADS, QK_NOPE + QK_ROPE)
            rope_cols = wq[:, :, QK_NOPE:]
            w_uq2 = jnp.concatenate([wq, _swap_halves(rope_cols)], axis=2)
            w_uq2_t = w_uq2.reshape(-1, N_HEADS * HEAD_W).T
            h, w_up, w_down = _mla(h, pos_row, freq_col, attn_norm, w_dq, q_latent_norm,
                                   w_uq2_t, w_o, *kv, ffn_w_up, ffn_w_down, layer, a)
        h = _ffn(h, ffn_norm, w_up, ffn_conv_w, ffn_conv_b, w_down, final_g,
                 layer=layer, final_norm=(layer == depth - 1))
        if layer == n_a - 1:
            kv = _shared_kv(h, pos_row, freq_col, row(kv_in_norm), w_dkv, row(kv_latent_norm),
                            w_kr2, w_uk, w_uv)
    return h
```

```python
import functools
import math

import jax
import jax.numpy as jnp
from jax import lax
from jax.experimental import pallas as pl
from jax.experimental.pallas import tpu as pltpu

CHUNK = 64
CONV_W = 3
N_HEADS = 8
QK_NOPE = 128
QK_ROPE = 64
V_HEAD = 128
ROPE_THETA = 10000.0
EPS = 1e-6
NEG_INF = -1e30

V7X_SUBLANES = 8
V7X_LANES = 128
V7X_MXU_DIM = 256
V7X_VMEM_BYTES = 64 * 1024 * 1024

HEAD_W = QK_NOPE + 2 * QK_ROPE
ROW_TILE = 512
KV_TILE = 2048
Q_TILE = 512
KEY_BLOCK = V7X_MXU_DIM
FF_CHUNK = V7X_MXU_DIM
VMEM_LIMIT = V7X_VMEM_BYTES - 8 * 1024 * 1024

_BF16 = jnp.bfloat16
_F32 = jnp.float32


def _dot(a, b):
    return jnp.dot(a, b, preferred_element_type=_F32)


def _rms(x, g):
    return x * lax.rsqrt(jnp.mean(x * x, axis=-1, keepdims=True) + EPS) * g


def _const_spec(shape):
    nd = len(shape)
    return pl.BlockSpec(shape, lambda *_: (0,) * nd, pipeline_mode=pl.Buffered(1))


def _layer_spec(shape, layer):
    nd = len(shape)
    return pl.BlockSpec((None,) + tuple(shape), lambda *_: (layer,) + (0,) * nd,
                        pipeline_mode=pl.Buffered(1))


def _row_spec(tile, width):
    return pl.BlockSpec((None, tile, width), lambda b, i: (b, i, 0))


def _cast_plan(w_up, w_down, layer, grid):
    bf16_rows = 2 * V7X_SUBLANES
    n_steps = grid[0] * grid[1]
    _, d, up_cols = w_up.shape
    _, dff, dn_cols = w_down.shape
    up_rows = d // n_steps
    every = next(k for k in range(1, n_steps + 1)
                 if n_steps % k == 0 and (dff * k) % (n_steps * bf16_rows) == 0)
    dn_rows = dff * every // n_steps
    assert up_rows * n_steps == d and up_rows % bf16_rows == 0
    step = lambda b, i: b * grid[1] + i
    in_specs = [pl.BlockSpec((None, up_rows, up_cols), lambda b, i: (layer, step(b, i), 0)),
                pl.BlockSpec((None, dn_rows, dn_cols), lambda b, i: (layer, step(b, i) // every, 0))]
    out_specs = [pl.BlockSpec((up_rows, up_cols), lambda b, i: (step(b, i), 0)),
                 pl.BlockSpec((dn_rows, dn_cols), lambda b, i: (step(b, i) // every, 0))]
    out_shape = [jax.ShapeDtypeStruct((d, up_cols), _BF16),
                 jax.ShapeDtypeStruct((dff, dn_cols), _BF16)]
    return in_specs, out_specs, out_shape


def _cast_block(src_up, src_dn, dst_up, dst_dn):
    dst_up[...] = src_up[...].astype(_BF16)
    dst_dn[...] = src_dn[...].astype(_BF16)


def _params():
    return pltpu.CompilerParams(
        dimension_semantics=("arbitrary", "arbitrary"), vmem_limit_bytes=VMEM_LIMIT)


def _causal_conv3(slab_ref, first, cur, w):
    halo, lanes = V7X_SUBLANES, slab_ref.shape[2]
    tile = cur.shape[0]
    outs = []
    for j in range(cur.shape[1] // lanes):
        cols = slice(j * lanes, (j + 1) * lanes)
        slab = slab_ref.at[first + j]
        slab[halo:halo + tile, :] = cur[:, cols]
        y = slab[pl.ds(halo - 2, tile), :] * w[0:1, cols]
        y = y + slab[pl.ds(halo - 1, tile), :] * w[1:2, cols]
        outs.append(y + cur[:, cols] * w[2:3, cols])
        slab[0:halo, :] = slab[tile:tile + halo, :]
    return jnp.concatenate(outs, axis=1)


def _mixer_kernel(x_ref, g_ref, win_ref, cw_ref, wout_ref, fup_ref, fdn_ref,
                  o_ref, fup_out, fdn_out, cu_ref, *, layer):
    tile, d = x_ref.shape
    halo = V7X_SUBLANES
    _cast_block(fup_ref, fdn_ref, fup_out, fdn_out)

    @pl.when(pl.program_id(1) == 0)
    def _():
        cu_ref[:, 0:halo, :] = jnp.zeros((cu_ref.shape[0], halo, cu_ref.shape[2]), _F32)

    x = x_ref[...]
    hn = _rms(x, g_ref[layer:layer + 1, :])
    b_gate = _dot(hn, win_ref[:, 0:d])
    cu = _dot(hn, win_ref[:, d:2 * d]) * _dot(hn, win_ref[:, 2 * d:3 * d])
    conv = _causal_conv3(cu_ref, 0, cu, cw_ref[...])
    o_ref[...] = x + _dot(b_gate * conv, wout_ref[...])


def _mixer(x, g, w_in, conv_w, w_out, ffn_w_up, ffn_w_down, layer, a):
    bsz, s, d = x.shape
    tile = ROW_TILE
    grid = (bsz, s // tile)
    cast_in, cast_out, cast_shape = _cast_plan(ffn_w_up, ffn_w_down, layer, grid)
    return pl.pallas_call(
        functools.partial(_mixer_kernel, layer=layer),
        grid=grid,
        in_specs=[_row_spec(tile, d), _const_spec(g.shape), _layer_spec((d, 3 * d), a),
                  _layer_spec((CONV_W, d), a), _layer_spec((d, d), a)] + cast_in,
        out_specs=[_row_spec(tile, d)] + cast_out,
        out_shape=[jax.ShapeDtypeStruct(x.shape, _F32)] + cast_shape,
        scratch_shapes=[pltpu.VMEM((d // V7X_LANES, tile + V7X_SUBLANES, V7X_LANES), _F32)],
        compiler_params=_params(),
        name="sc_mixer",
    )(x, g, w_in, conv_w, w_out, ffn_w_up, ffn_w_down)


def _ffn_kernel(h_ref, g_ref, wup_ref, cw_ref, cb_ref, wdn_ref, fg_ref, o_ref,
                gbuf_ref, act_ref, *, layer, final_norm):
    tile, d = h_ref.shape
    dff = act_ref.shape[1]
    halo = V7X_SUBLANES
    ch = FF_CHUNK

    @pl.when(pl.program_id(1) == 0)
    def _():
        gbuf_ref[:, 0:halo, :] = jnp.zeros((gbuf_ref.shape[0], halo, gbuf_ref.shape[2]), _F32)

    h = h_ref[...]
    hn = _rms(h, g_ref[layer:layer + 1, :]).astype(_BF16)
    for c in range(dff // ch):
        cols = slice(c * ch, (c + 1) * ch)
        g = _dot(hn, wup_ref[:, cols])
        v = _dot(hn, wup_ref[:, dff + c * ch:dff + (c + 1) * ch])
        pre = _causal_conv3(gbuf_ref, c, g, cw_ref[:, cols]) + cb_ref[layer:layer + 1, cols]
        act = pre * (1.0 / (1.0 + jnp.exp(-pre))) * v
        act_ref[:, cols] = act.astype(_BF16)
    out = h + _dot(act_ref[...], wdn_ref[...])
    if final_norm:
        out = _rms(out, fg_ref[...])
    o_ref[...] = out


def _ffn(h, g, w_up, conv_w, conv_b, w_down, final_g, *, layer, final_norm):
    bsz, s, d = h.shape
    dff = w_down.shape[0]
    tile = ROW_TILE
    return pl.pallas_call(
        functools.partial(_ffn_kernel, layer=layer, final_norm=final_norm),
        grid=(bsz, s // tile),
        in_specs=[_row_spec(tile, d), _const_spec(g.shape), _const_spec((d, 2 * dff)),
                  _layer_spec((CONV_W, dff), layer), _const_spec(conv_b.shape),
                  _const_spec((dff, d)), _const_spec((1, d))],
        out_specs=_row_spec(tile, d),
        out_shape=jax.ShapeDtypeStruct(h.shape, _F32),
        scratch_shapes=[pltpu.VMEM((dff // FF_CHUNK, tile + V7X_SUBLANES, FF_CHUNK), _F32),
                        pltpu.VMEM((tile, dff), _BF16)],
        compiler_params=_params(),
        name="conv_ffn_final" if final_norm else "conv_ffn",
    )(h, g, w_up, conv_w, conv_b, w_down, final_g)


def _rope_table_t(pos_row, inv_freq_col, scale):
    ang = pos_row.astype(_F32) * inv_freq_col
    cos, sin = jnp.cos(ang), jnp.sin(ang)
    table = jnp.concatenate([cos, cos, -sin, sin], axis=0)
    return table * scale if scale != 1.0 else table


def _kv_kernel(h_ref, pos_ref, freq_ref, gin_ref, wdkv_ref, glat_ref, wkr_ref, wuk_ref,
               wuv_ref, kn_ref, kr_ref, vt_ref):
    hn = _rms(h_ref[...], gin_ref[...])
    c_kv = _rms(_dot(hn, wdkv_ref[...]), glat_ref[...])
    p = _dot(hn, wkr_ref[...]) * _rope_table_t(pos_ref[...], freq_ref[...], 1.0).T
    kr_ref[...] = (p + pltpu.roll(p, QK_ROPE, 1)).astype(_BF16)
    kn_ref[...] = _dot(c_kv, wuk_ref[...]).astype(_BF16)
    vt_ref[...] = _dot(wuv_ref[...].T, c_kv.T).astype(_BF16)


def _shared_kv(h, pos, freq, g_in, w_dkv, g_lat, w_kr2, w_uk, w_uv):
    bsz, s, d = h.shape
    lora = w_dkv.shape[1]
    hk = w_uk.shape[1]
    hv = w_uv.shape[1]
    tile = KV_TILE
    return pl.pallas_call(
        _kv_kernel,
        grid=(bsz, s // tile),
        in_specs=[_row_spec(tile, d), pl.BlockSpec((None, 1, tile), lambda b, i: (b, 0, i)),
                  _const_spec((QK_ROPE // 2, 1)),
                  _const_spec((1, d)), _const_spec((d, lora)), _const_spec((1, lora)),
                  _const_spec((d, 2 * QK_ROPE)), _const_spec((lora, hk)),
                  _const_spec((lora, hv))],
        out_specs=[_row_spec(tile, hk), _row_spec(tile, 2 * QK_ROPE),
                   pl.BlockSpec((None, hv, tile), lambda b, i: (b, 0, i))],
        out_shape=[jax.ShapeDtypeStruct((bsz, s, hk), _BF16),
                   jax.ShapeDtypeStruct((bsz, s, 2 * QK_ROPE), _BF16),
                   jax.ShapeDtypeStruct((bsz, hv, s), _BF16)],
        compiler_params=_params(),
        name="shared_kv",
    )(h, pos, freq, g_in, w_dkv, g_lat, w_kr2, w_uk, w_uv)


def _mla_kernel(h_ref, posr_ref, freqc_ref, g_ref, wdq_ref, gq_ref, wuqt_ref, wo_ref,
                kn_ref, kr_ref, vt_ref, fup_ref, fdn_ref, o_ref, fup_out, fdn_out,
                qt_ref, ob_ref, s_ref, *, layer, a):
    tq, d = h_ref.shape
    i = pl.program_id(1)
    _cast_block(fup_ref, fdn_ref, fup_out, fdn_out)
    scale = float(QK_NOPE + QK_ROPE) ** -0.5 * math.log2(math.e)

    h = h_ref[...]
    hn = _rms(h, g_ref[layer:layer + 1, :])
    c_q = _rms(_dot(hn, wdq_ref[...]), gq_ref[a:a + 1, :])
    c_qt = c_q.T
    table = _rope_table_t(posr_ref[...], freqc_ref[...], scale)
    for hd in range(N_HEADS):
        qt = _dot(wuqt_ref[hd * HEAD_W:(hd + 1) * HEAD_W, :], c_qt)
        qt_ref[hd * HEAD_W:hd * HEAD_W + QK_NOPE, :] = (qt[0:QK_NOPE] * scale).astype(_BF16)
        qt_ref[hd * HEAD_W + QK_NOPE:(hd + 1) * HEAD_W, :] = (qt[QK_NOPE:] * table).astype(_BF16)

    ones_rows = 2 * V7X_SUBLANES
    kb = KEY_BLOCK
    n_sub = tq // kb
    k_chunk = lax.broadcasted_iota(jnp.int32, (kb, kb), 0) // CHUNK
    q_chunk = lax.broadcasted_iota(jnp.int32, (kb, kb), 1) // CHUNK
    diag_mask = k_chunk <= q_chunk
    ones = jnp.ones((ones_rows, kb), _BF16)

    def attend(c):
        n_blocks = (c + 1) * n_sub

        def first_lane(t):
            return max(t - c * n_sub, 0) * kb

        def score_block(hd, t, m8):
            rows = slice(t * kb, (t + 1) * kb)
            lo = first_lane(t)
            k = jnp.concatenate([kn_ref[rows, hd * QK_NOPE:(hd + 1) * QK_NOPE], kr_ref[rows, :]],
                                axis=1)
            s = _dot(k, qt_ref[hd * HEAD_W:(hd + 1) * HEAD_W, lo:tq])
            if t >= c * n_sub:
                s_diag = jnp.where(diag_mask, s[:, 0:kb], NEG_INF)
                s = jnp.concatenate([s_diag, s[:, kb:]], axis=1) if lo + kb < tq else s_diag
            s_ref[hd % 2, rows, lo:tq] = s
            blk_max = jnp.max(s.reshape(kb // V7X_SUBLANES, V7X_SUBLANES, tq - lo), axis=0)
            if m8 is None:
                return blk_max
            if lo == 0:
                return jnp.maximum(m8, blk_max)
            return jnp.concatenate([m8[:, 0:lo], jnp.maximum(m8[:, lo:], blk_max)], axis=1)

        def pv_block(hd, t, m, acc):
            rows = slice(t * kb, (t + 1) * kb)
            lo = first_lane(t)
            p = jnp.exp2(s_ref[hd % 2, rows, lo:tq] - m[:, lo:]).astype(_BF16)
            v_aug = jnp.concatenate([vt_ref[hd * V_HEAD:(hd + 1) * V_HEAD, rows], ones], axis=0)
            o_aug = _dot(v_aug, p)
            if acc is None:
                return o_aug
            if lo == 0:
                return acc + o_aug
            return jnp.concatenate([acc[:, 0:lo], acc[:, lo:] + o_aug], axis=1)

        m8_next = None
        for t in range(n_blocks):
            m8_next = score_block(0, t, m8_next)
        for hd in range(N_HEADS):
            m = jnp.max(m8_next, axis=0, keepdims=True)
            m8_next, acc = None, None
            for t in range(n_blocks):
                if hd + 1 < N_HEADS:
                    m8_next = score_block(hd + 1, t, m8_next)
                acc = pv_block(hd, t, m, acc)
            o_t = acc[0:V_HEAD] * (1.0 / acc[V_HEAD:V_HEAD + 1])
            ob_ref[:, hd * V_HEAD:(hd + 1) * V_HEAD] = o_t.T

    for c in range(kn_ref.shape[0] // tq):
        pl.when(i == c)(functools.partial(attend, c))

    o_ref[...] = h + _dot(ob_ref[...], wo_ref[...])


def _mla(h, pos_row, freq_col, g, w_dq, g_q, w_uq2_t, w_o, k_nope, k_rope, v_t,
         ffn_w_up, ffn_w_down, layer, a):
    bsz, s, d = h.shape
    qlora = w_dq.shape[2]
    hk = k_nope.shape[2]
    hv = v_t.shape[1]
    tq = Q_TILE
    grid = (bsz, s // tq)
    cast_in, cast_out, cast_shape = _cast_plan(ffn_w_up, ffn_w_down, layer, grid)
    return pl.pallas_call(
        functools.partial(_mla_kernel, layer=layer, a=a),
        grid=grid,
        in_specs=[_row_spec(tq, d), pl.BlockSpec((None, 1, tq), lambda b, i: (b, 0, i)),
                  _const_spec((QK_ROPE // 2, 1)),
                  _const_spec(g.shape), _layer_spec((d, qlora), a), _const_spec(g_q.shape),
                  _const_spec((N_HEADS * HEAD_W, qlora)), _layer_spec((hv, d), a),
                  pl.BlockSpec((None, s, hk), lambda b, i: (b, 0, 0)),
                  pl.BlockSpec((None, s, 2 * QK_ROPE), lambda b, i: (b, 0, 0)),
                  pl.BlockSpec((None, hv, s), lambda b, i: (b, 0, 0))] + cast_in,
        out_specs=[_row_spec(tq, d)] + cast_out,
        out_shape=[jax.ShapeDtypeStruct(h.shape, _F32)] + cast_shape,
        scratch_shapes=[pltpu.VMEM((N_HEADS * HEAD_W, tq), _BF16),
                        pltpu.VMEM((tq, hv), _F32),
                        pltpu.VMEM((2, s, tq), _F32)],
        compiler_params=_params(),
        name="mla_attention",
    )(h, pos_row, freq_col, g, w_dq, g_q, w_uq2_t, w_o, k_nope, k_rope, v_t,
      ffn_w_up, ffn_w_down)


def _swap_halves(w):
    half = w.shape[-1] // 2
    return jnp.concatenate([w[..., half:], w[..., :half]], axis=-1)


def kernel(x, positions, attn_norm, ffn_norm, final_norm, sc_w_in, sc_conv_w, sc_w_out, kv_in_norm, w_dkv, kv_latent_norm, w_kr, w_uk, w_uv, w_dq, q_latent_norm, w_uq, w_o, ffn_w_up, ffn_conv_w, ffn_conv_b, ffn_w_down):
    depth = attn_norm.shape[0]
    n_a = sc_w_in.shape[0]
    row = lambda v: v.reshape(1, -1)

    half = QK_ROPE // 2
    inv_freq = 1.0 / (ROPE_THETA ** (jnp.arange(half, dtype=_F32) / half))
    freq_col = inv_freq.reshape(half, 1)
    pos_row = positions.reshape(positions.shape[0], 1, positions.shape[1])

    w_kr2 = jnp.concatenate([w_kr, _swap_halves(w_kr)], axis=1)
    final_g = row(final_norm)

    h = x
    kv = None
    for layer in range(depth):
        if layer < n_a:
            h, w_up, w_down = _mixer(h, attn_norm, sc_w_in, sc_conv_w, sc_w_out,
                                     ffn_w_up, ffn_w_down, layer, layer)
        else:
            a = layer - n_a
            wq = w_uq[a].reshape(-1, N_HEADS, QK_NOPE + QK_ROPE)
            rope_cols = wq[:, :, QK_NOPE:]
            w_uq2 = jnp.concatenate([wq, _swap_halves(rope_cols)], axis=2)
            w_uq2_t = w_uq2.reshape(-1, N_HEADS * HEAD_W).T
            h, w_up, w_down = _mla(h, pos_row, freq_col, attn_norm, w_dq, q_latent_norm,
                                   w_uq2_t, w_o, *kv, ffn_w_up, ffn_w_down, layer, a)
        h = _ffn(h, ffn_norm, w_up, ffn_conv_w, ffn_conv_b, w_down, final_g,
                 layer=layer, final_norm=(layer == depth - 1))
        if layer == n_a - 1:
            kv = _shared_kv(h, pos_row, freq_col, row(kv_in_norm), w_dkv, row(kv_latent_norm),
                            w_kr2, w_uk, w_uv)
    return h
```

```python
import functools
import math

import jax
import jax.numpy as jnp
from jax import lax
from jax.experimental import pallas as pl
from jax.experimental.pallas import tpu as pltpu

CHUNK = 64
CONV_W = 3
N_HEADS = 8
QK_NOPE = 128
QK_ROPE = 64
V_HEAD = 128
ROPE_THETA = 10000.0
EPS = 1e-6
NEG_INF = -1e30

V7X_SUBLANES = 8
V7X_LANES = 128
V7X_MXU_DIM = 256
V7X_VMEM_BYTES = 64 * 1024 * 1024

HEAD_W = QK_NOPE + 2 * QK_ROPE
ROW_TILE = 512
MIXER_TILE = 1024
KV_TILE = 2048
Q_TILE = 512
KEY_BLOCK = V7X_MXU_DIM
FF_CHUNK = V7X_MXU_DIM
VMEM_LIMIT = V7X_VMEM_BYTES - 8 * 1024 * 1024

_BF16 = jnp.bfloat16
_F32 = jnp.float32


def _dot(a, b):
    return jnp.dot(a, b, preferred_element_type=_F32)


def _rms(x, g):
    return x * lax.rsqrt(jnp.mean(x * x, axis=-1, keepdims=True) + EPS) * g


def _const_spec(shape):
    nd = len(shape)
    return pl.BlockSpec(shape, lambda *_: (0,) * nd, pipeline_mode=pl.Buffered(1))


def _layer_spec(shape, layer):
    nd = len(shape)
    return pl.BlockSpec((None,) + tuple(shape), lambda *_: (layer,) + (0,) * nd,
                        pipeline_mode=pl.Buffered(1))


def _row_spec(tile, width):
    return pl.BlockSpec((None, tile, width), lambda b, i: (b, i, 0))


def _cast_plan(w_up, w_down, layer, grid):
    bf16_rows = 2 * V7X_SUBLANES
    n_steps = grid[0] * grid[1]
    _, d, up_cols = w_up.shape
    _, dff, dn_cols = w_down.shape
    up_rows = d // n_steps
    every = next(k for k in range(1, n_steps + 1)
                 if n_steps % k == 0 and (dff * k) % (n_steps * bf16_rows) == 0)
    dn_rows = dff * every // n_steps
    assert up_rows * n_steps == d and up_rows % bf16_rows == 0
    step = lambda b, i: b * grid[1] + i
    in_specs = [pl.BlockSpec((None, up_rows, up_cols), lambda b, i: (layer, step(b, i), 0)),
                pl.BlockSpec((None, dn_rows, dn_cols), lambda b, i: (layer, step(b, i) // every, 0))]
    out_specs = [pl.BlockSpec((up_rows, up_cols), lambda b, i: (step(b, i), 0)),
                 pl.BlockSpec((dn_rows, dn_cols), lambda b, i: (step(b, i) // every, 0))]
    out_shape = [jax.ShapeDtypeStruct((d, up_cols), _BF16),
                 jax.ShapeDtypeStruct((dff, dn_cols), _BF16)]
    return in_specs, out_specs, out_shape


def _cast_block(src_up, src_dn, dst_up, dst_dn):
    dst_up[...] = src_up[...].astype(_BF16)
    dst_dn[...] = src_dn[...].astype(_BF16)


def _params():
    return pltpu.CompilerParams(
        dimension_semantics=("arbitrary", "arbitrary"), vmem_limit_bytes=VMEM_LIMIT)


def _causal_conv3(slab_ref, first, cur, w):
    halo, lanes = V7X_SUBLANES, slab_ref.shape[2]
    tile = cur.shape[0]
    outs = []
    for j in range(cur.shape[1] // lanes):
        cols = slice(j * lanes, (j + 1) * lanes)
        slab = slab_ref.at[first + j]
        slab[halo:halo + tile, :] = cur[:, cols]
        y = slab[pl.ds(halo - 2, tile), :] * w[0:1, cols]
        y = y + slab[pl.ds(halo - 1, tile), :] * w[1:2, cols]
        outs.append(y + cur[:, cols] * w[2:3, cols])
        slab[0:halo, :] = slab[tile:tile + halo, :]
    return jnp.concatenate(outs, axis=1)


def _mixer_kernel(x_ref, g_ref, win_ref, cw_ref, wout_ref, fup_ref, fdn_ref,
                  o_ref, fup_out, fdn_out, cu_ref, *, layer):
    tile, d = x_ref.shape
    halo = V7X_SUBLANES
    _cast_block(fup_ref, fdn_ref, fup_out, fdn_out)

    @pl.when(pl.program_id(1) == 0)
    def _():
        cu_ref[:, 0:halo, :] = jnp.zeros((cu_ref.shape[0], halo, cu_ref.shape[2]), _F32)

    x = x_ref[...]
    hn = _rms(x, g_ref[layer:layer + 1, :])
    b_gate = _dot(hn, win_ref[:, 0:d])
    cu = _dot(hn, win_ref[:, d:2 * d]) * _dot(hn, win_ref[:, 2 * d:3 * d])
    conv = _causal_conv3(cu_ref, 0, cu, cw_ref[...])
    o_ref[...] = x + _dot(b_gate * conv, wout_ref[...])


def _mixer(x, g, w_in, conv_w, w_out, ffn_w_up, ffn_w_down, layer, a):
    bsz, s, d = x.shape
    tile = MIXER_TILE
    grid = (bsz, s // tile)
    cast_in, cast_out, cast_shape = _cast_plan(ffn_w_up, ffn_w_down, layer, grid)
    return pl.pallas_call(
        functools.partial(_mixer_kernel, layer=layer),
        grid=grid,
        in_specs=[_row_spec(tile, d), _const_spec(g.shape), _layer_spec((d, 3 * d), a),
                  _layer_spec((CONV_W, d), a), _layer_spec((d, d), a)] + cast_in,
        out_specs=[_row_spec(tile, d)] + cast_out,
        out_shape=[jax.ShapeDtypeStruct(x.shape, _F32)] + cast_shape,
        scratch_shapes=[pltpu.VMEM((d // V7X_LANES, tile + V7X_SUBLANES, V7X_LANES), _F32)],
        compiler_params=_params(),
        name="sc_mixer",
    )(x, g, w_in, conv_w, w_out, ffn_w_up, ffn_w_down)


def _ffn_kernel(h_ref, g_ref, wup_ref, cw_ref, cb_ref, wdn_ref, fg_ref, o_ref,
                gbuf_ref, act_ref, *, layer, final_norm):
    tile, d = h_ref.shape
    dff = act_ref.shape[1]
    halo = V7X_SUBLANES
    ch = FF_CHUNK

    @pl.when(pl.program_id(1) == 0)
    def _():
        gbuf_ref[:, 0:halo, :] = jnp.zeros((gbuf_ref.shape[0], halo, gbuf_ref.shape[2]), _F32)

    h = h_ref[...]
    hn = _rms(h, g_ref[layer:layer + 1, :]).astype(_BF16)
    for c in range(dff // ch):
        cols = slice(c * ch, (c + 1) * ch)
        g = _dot(hn, wup_ref[:, cols])
        v = _dot(hn, wup_ref[:, dff + c * ch:dff + (c + 1) * ch])
        pre = _causal_conv3(gbuf_ref, c, g, cw_ref[:, cols]) + cb_ref[layer:layer + 1, cols]
        act = pre * (1.0 / (1.0 + jnp.exp(-pre))) * v
        act_ref[:, cols] = act.astype(_BF16)
    out = h + _dot(act_ref[...], wdn_ref[...])
    if final_norm:
        out = _rms(out, fg_ref[...])
    o_ref[...] = out


def _ffn(h, g, w_up, conv_w, conv_b, w_down, final_g, *, layer, final_norm):
    bsz, s, d = h.shape
    dff = w_down.shape[0]
    tile = ROW_TILE
    return pl.pallas_call(
        functools.partial(_ffn_kernel, layer=layer, final_norm=final_norm),
        grid=(bsz, s // tile),
        in_specs=[_row_spec(tile, d), _const_spec(g.shape), _const_spec((d, 2 * dff)),
                  _layer_spec((CONV_W, dff), layer), _const_spec(conv_b.shape),
                  _const_spec((dff, d)), _const_spec((1, d))],
        out_specs=_row_spec(tile, d),
        out_shape=jax.ShapeDtypeStruct(h.shape, _F32),
        scratch_shapes=[pltpu.VMEM((dff // FF_CHUNK, tile + V7X_SUBLANES, FF_CHUNK), _F32),
                        pltpu.VMEM((tile, dff), _BF16)],
        compiler_params=_params(),
        name="conv_ffn_final" if final_norm else "conv_ffn",
    )(h, g, w_up, conv_w, conv_b, w_down, final_g)


def _rope_table_t(pos_row, inv_freq_col, scale):
    ang = pos_row.astype(_F32) * inv_freq_col
    cos, sin = jnp.cos(ang), jnp.sin(ang)
    table = jnp.concatenate([cos, cos, -sin, sin], axis=0)
    return table * scale if scale != 1.0 else table


def _kv_kernel(h_ref, pos_ref, freq_ref, gin_ref, wdkv_ref, glat_ref, wkr_ref, wuk_ref,
               wuv_ref, kn_ref, kr_ref, vt_ref):
    hn = _rms(h_ref[...], gin_ref[...])
    c_kv = _rms(_dot(hn, wdkv_ref[...]), glat_ref[...])
    p = _dot(hn, wkr_ref[...]) * _rope_table_t(pos_ref[...], freq_ref[...], 1.0).T
    kr_ref[...] = (p + pltpu.roll(p, QK_ROPE, 1)).astype(_BF16)
    kn_ref[...] = _dot(c_kv, wuk_ref[...]).astype(_BF16)
    vt_ref[...] = _dot(wuv_ref[...].T, c_kv.T).astype(_BF16)


def _shared_kv(h, pos, freq, g_in, w_dkv, g_lat, w_kr2, w_uk, w_uv):
    bsz, s, d = h.shape
    lora = w_dkv.shape[1]
    hk = w_uk.shape[1]
    hv = w_uv.shape[1]
    tile = KV_TILE
    return pl.pallas_call(
        _kv_kernel,
        grid=(bsz, s // tile),
        in_specs=[_row_spec(tile, d), pl.BlockSpec((None, 1, tile), lambda b, i: (b, 0, i)),
                  _const_spec((QK_ROPE // 2, 1)),
                  _const_spec((1, d)), _const_spec((d, lora)), _const_spec((1, lora)),
                  _const_spec((d, 2 * QK_ROPE)), _const_spec((lora, hk)),
                  _const_spec((lora, hv))],
        out_specs=[_row_spec(tile, hk), _row_spec(tile, 2 * QK_ROPE),
                   pl.BlockSpec((None, hv, tile), lambda b, i: (b, 0, i))],
        out_shape=[jax.ShapeDtypeStruct((bsz, s, hk), _BF16),
                   jax.ShapeDtypeStruct((bsz, s, 2 * QK_ROPE), _BF16),
                   jax.ShapeDtypeStruct((bsz, hv, s), _BF16)],
        compiler_params=_params(),
        name="shared_kv",
    )(h, pos, freq, g_in, w_dkv, g_lat, w_kr2, w_uk, w_uv)


def _mla_kernel(h_ref, posr_ref, freqc_ref, g_ref, wdq_ref, gq_ref, wuqt_ref, wo_ref,
                kn_ref, kr_ref, vt_ref, fup_ref, fdn_ref, o_ref, fup_out, fdn_out,
                qt_ref, ob_ref, s_ref, *, layer, a):
    tq, d = h_ref.shape
    i = pl.program_id(1)
    _cast_block(fup_ref, fdn_ref, fup_out, fdn_out)
    scale = float(QK_NOPE + QK_ROPE) ** -0.5 * math.log2(math.e)

    h = h_ref[...]
    hn = _rms(h, g_ref[layer:layer + 1, :])
    c_q = _rms(_dot(hn, wdq_ref[...]), gq_ref[a:a + 1, :])
    c_qt = c_q.T
    table = _rope_table_t(posr_ref[...], freqc_ref[...], scale)
    for hd in range(N_HEADS):
        qt = _dot(wuqt_ref[hd * HEAD_W:(hd + 1) * HEAD_W, :], c_qt)
        qt_ref[hd * HEAD_W:hd * HEAD_W + QK_NOPE, :] = (qt[0:QK_NOPE] * scale).astype(_BF16)
        qt_ref[hd * HEAD_W + QK_NOPE:(hd + 1) * HEAD_W, :] = (qt[QK_NOPE:] * table).astype(_BF16)

    ones_rows = 2 * V7X_SUBLANES
    kb = KEY_BLOCK
    n_sub = tq // kb
    k_chunk = lax.broadcasted_iota(jnp.int32, (kb, kb), 0) // CHUNK
    q_chunk = lax.broadcasted_iota(jnp.int32, (kb, kb), 1) // CHUNK
    diag_mask = k_chunk <= q_chunk
    ones = jnp.ones((ones_rows, kb), _BF16)

    def attend(c):
        n_blocks = (c + 1) * n_sub

        def first_lane(t):
            return max(t - c * n_sub, 0) * kb

        def score_block(hd, t, m8):
            rows = slice(t * kb, (t + 1) * kb)
            lo = first_lane(t)
            k = jnp.concatenate([kn_ref[rows, hd * QK_NOPE:(hd + 1) * QK_NOPE], kr_ref[rows, :]],
                                axis=1)
            s = _dot(k, qt_ref[hd * HEAD_W:(hd + 1) * HEAD_W, lo:tq])
            if t >= c * n_sub:
                s_diag = jnp.where(diag_mask, s[:, 0:kb], NEG_INF)
                s = jnp.concatenate([s_diag, s[:, kb:]], axis=1) if lo + kb < tq else s_diag
            s_ref[hd % 2, rows, lo:tq] = s
            blk_max = jnp.max(s.reshape(kb // V7X_SUBLANES, V7X_SUBLANES, tq - lo), axis=0)
            if m8 is None:
                return blk_max
            if lo == 0:
                return jnp.maximum(m8, blk_max)
            return jnp.concatenate([m8[:, 0:lo], jnp.maximum(m8[:, lo:], blk_max)], axis=1)

        def pv_block(hd, t, m, acc):
            rows = slice(t * kb, (t + 1) * kb)
            lo = first_lane(t)
            p = jnp.exp2(s_ref[hd % 2, rows, lo:tq] - m[:, lo:]).astype(_BF16)
            v_aug = jnp.concatenate([vt_ref[hd * V_HEAD:(hd + 1) * V_HEAD, rows], ones], axis=0)
            o_aug = _dot(v_aug, p)
            if acc is None:
                return o_aug
            if lo == 0:
                return acc + o_aug
            return jnp.concatenate([acc[:, 0:lo], acc[:, lo:] + o_aug], axis=1)

        m8_next = None
        for t in range(n_blocks):
            m8_next = score_block(0, t, m8_next)
        for hd in range(N_HEADS):
            m = jnp.max(m8_next, axis=0, keepdims=True)
            m8_next, acc = None, None
            for t in range(n_blocks):
                if hd + 1 < N_HEADS:
                    m8_next = score_block(hd + 1, t, m8_next)
                acc = pv_block(hd, t, m, acc)
            o_t = acc[0:V_HEAD] * (1.0 / acc[V_HEAD:V_HEAD + 1])
            ob_ref[:, hd * V_HEAD:(hd + 1) * V_HEAD] = o_t.T

    for c in range(kn_ref.shape[0] // tq):
        pl.when(i == c)(functools.partial(attend, c))

    o_ref[...] = h + _dot(ob_ref[...], wo_ref[...])


def _mla(h, pos_row, freq_col, g, w_dq, g_q, w_uq2_t, w_o, k_nope, k_rope, v_t,
         ffn_w_up, ffn_w_down, layer, a):
    bsz, s, d = h.shape
    qlora = w_dq.shape[2]
    hk = k_nope.shape[2]
    hv = v_t.shape[1]
    tq = Q_TILE
    grid = (bsz, s // tq)
    cast_in, cast_out, cast_shape = _cast_plan(ffn_w_up, ffn_w_down, layer, grid)
    return pl.pallas_call(
        functools.partial(_mla_kernel, layer=layer, a=a),
        grid=grid,
        in_specs=[_row_spec(tq, d), pl.BlockSpec((None, 1, tq), lambda b, i: (b, 0, i)),
                  _const_spec((QK_ROPE // 2, 1)),
                  _const_spec(g.shape), _layer_spec((d, qlora), a), _const_spec(g_q.shape),
                  _const_spec((N_HEADS * HEAD_W, qlora)), _layer_spec((hv, d), a),
                  pl.BlockSpec((None, s, hk), lambda b, i: (b, 0, 0)),
                  pl.BlockSpec((None, s, 2 * QK_ROPE), lambda b, i: (b, 0, 0)),
                  pl.BlockSpec((None, hv, s), lambda b, i: (b, 0, 0))] + cast_in,
        out_specs=[_row_spec(tq, d)] + cast_out,
        out_shape=[jax.ShapeDtypeStruct(h.shape, _F32)] + cast_shape,
        scratch_shapes=[pltpu.VMEM((N_HEADS * HEAD_W, tq), _BF16),
                        pltpu.VMEM((tq, hv), _F32),
                        pltpu.VMEM((2, s, tq), _F32)],
        compiler_params=_params(),
        name="mla_attention",
    )(h, pos_row, freq_col, g, w_dq, g_q, w_uq2_t, w_o, k_nope, k_rope, v_t,
      ffn_w_up, ffn_w_down)


def _swap_halves(w):
    half = w.shape[-1] // 2
    return jnp.concatenate([w[..., half:], w[..., :half]], axis=-1)


def kernel(x, positions, attn_norm, ffn_norm, final_norm, sc_w_in, sc_conv_w, sc_w_out, kv_in_norm, w_dkv, kv_latent_norm, w_kr, w_uk, w_uv, w_dq, q_latent_norm, w_uq, w_o, ffn_w_up, ffn_conv_w, ffn_conv_b, ffn_w_down):
    depth = attn_norm.shape[0]
    n_a = sc_w_in.shape[0]
    row = lambda v: v.reshape(1, -1)

    half = QK_ROPE // 2
    inv_freq = 1.0 / (ROPE_THETA ** (jnp.arange(half, dtype=_F32) / half))
    freq_col = inv_freq.reshape(half, 1)
    pos_row = positions.reshape(positions.shape[0], 1, positions.shape[1])

    w_kr2 = jnp.concatenate([w_kr, _swap_halves(w_kr)], axis=1)
    final_g = row(final_norm)

    h = x
    kv = None
    for layer in range(depth):
        if layer < n_a:
            h, w_up, w_down = _mixer(h, attn_norm, sc_w_in, sc_conv_w, sc_w_out,
                                     ffn_w_up, ffn_w_down, layer, layer)
        else:
            a = layer - n_a
            wq = w_uq[a].reshape(-1, N_HEADS, QK_NOPE + QK_ROPE)
            rope_cols = wq[:, :, QK_NOPE:]
            w_uq2 = jnp.concatenate([wq, _swap_halves(rope_cols)], axis=2)
            w_uq2_t = w_uq2.reshape(-1, N_HEADS * HEAD_W).T
            h, w_up, w_down = _mla(h, pos_row, freq_col, attn_norm, w_dq, q_latent_norm,
                                   w_uq2_t, w_o, *kv, ffn_w_up, ffn_w_down, layer, a)
        h = _ffn(h, ffn_norm, w_up, ffn_conv_w, ffn_conv_b, w_down, final_g,
                 layer=layer, final_norm=(layer == depth - 1))
        if layer == n_a - 1:
            kv = _shared_kv(h, pos_row, freq_col, row(kv_in_norm), w_dkv, row(kv_latent_norm),
                            w_kr2, w_uk, w_uv)
    return h
```

```python
import functools
import math

import jax
import jax.numpy as jnp
from jax import lax
from jax.experimental import pallas as pl
from jax.experimental.pallas import tpu as pltpu

CHUNK = 64
CONV_W = 3
N_HEADS = 8
QK_NOPE = 128
QK_ROPE = 64
V_HEAD = 128
ROPE_THETA = 10000.0
EPS = 1e-6
NEG_INF = -1e30

V7X_SUBLANES = 8
V7X_LANES = 128
V7X_MXU_DIM = 256
V7X_VMEM_BYTES = 64 * 1024 * 1024

HEAD_W = QK_NOPE + 2 * QK_ROPE
ROW_TILE = 512
KV_TILE = 2048
Q_TILE = 512
KEY_BLOCK = V7X_MXU_DIM
FF_CHUNK = V7X_MXU_DIM
MIB = 1024 * 1024
MIXER_VMEM, FFN_VMEM, KV_VMEM, MLA_VMEM = 40 * MIB, 38 * MIB, 45 * MIB, 56 * MIB
assert max(MIXER_VMEM, FFN_VMEM, KV_VMEM, MLA_VMEM) <= V7X_VMEM_BYTES - 8 * MIB

_BF16 = jnp.bfloat16
_F32 = jnp.float32


def _dot(a, b):
    return jnp.dot(a, b, preferred_element_type=_F32)


def _rms(x, g):
    return x * lax.rsqrt(jnp.mean(x * x, axis=-1, keepdims=True) + EPS) * g


def _const_spec(shape):
    nd = len(shape)
    return pl.BlockSpec(shape, lambda *_: (0,) * nd, pipeline_mode=pl.Buffered(1))


def _layer_spec(shape, layer):
    nd = len(shape)
    return pl.BlockSpec((None,) + tuple(shape), lambda *_: (layer,) + (0,) * nd,
                        pipeline_mode=pl.Buffered(1))


def _row_spec(tile, width):
    return pl.BlockSpec((None, tile, width), lambda b, i: (b, i, 0))


def _cast_plan(w_up, w_down, layer, grid):
    bf16_rows = 2 * V7X_SUBLANES
    n_steps = grid[0] * grid[1]
    _, d, up_cols = w_up.shape
    _, dff, dn_cols = w_down.shape
    up_rows = d // n_steps
    every = next(k for k in range(1, n_steps + 1)
                 if n_steps % k == 0 and (dff * k) % (n_steps * bf16_rows) == 0)
    dn_rows = dff * every // n_steps
    assert up_rows * n_steps == d and up_rows % bf16_rows == 0
    step = lambda b, i: b * grid[1] + i
    in_specs = [pl.BlockSpec((None, up_rows, up_cols), lambda b, i: (layer, step(b, i), 0)),
                pl.BlockSpec((None, dn_rows, dn_cols), lambda b, i: (layer, step(b, i) // every, 0))]
    out_specs = [pl.BlockSpec((up_rows, up_cols), lambda b, i: (step(b, i), 0)),
                 pl.BlockSpec((dn_rows, dn_cols), lambda b, i: (step(b, i) // every, 0))]
    out_shape = [jax.ShapeDtypeStruct((d, up_cols), _BF16),
                 jax.ShapeDtypeStruct((dff, dn_cols), _BF16)]
    return in_specs, out_specs, out_shape


def _cast_block(src_up, src_dn, dst_up, dst_dn):
    dst_up[...] = src_up[...].astype(_BF16)
    dst_dn[...] = src_dn[...].astype(_BF16)


def _params(vmem_bytes):
    return pltpu.CompilerParams(
        dimension_semantics=("arbitrary", "arbitrary"), vmem_limit_bytes=vmem_bytes)


def _causal_conv3(slab_ref, first, cur, w):
    halo, lanes = V7X_SUBLANES, slab_ref.shape[2]
    tile = cur.shape[0]
    outs = []
    for j in range(cur.shape[1] // lanes):
        cols = slice(j * lanes, (j + 1) * lanes)
        slab = slab_ref.at[first + j]
        slab[halo:halo + tile, :] = cur[:, cols]
        y = slab[pl.ds(halo - 2, tile), :] * w[0:1, cols]
        y = y + slab[pl.ds(halo - 1, tile), :] * w[1:2, cols]
        outs.append(y + cur[:, cols] * w[2:3, cols])
        slab[0:halo, :] = slab[tile:tile + halo, :]
    return jnp.concatenate(outs, axis=1)


def _mixer_kernel(x_ref, g_ref, win_ref, cw_ref, wout_ref, fup_ref, fdn_ref,
                  o_ref, fup_out, fdn_out, cu_ref, *, layer):
    tile, d = x_ref.shape
    halo = V7X_SUBLANES
    _cast_block(fup_ref, fdn_ref, fup_out, fdn_out)

    @pl.when(pl.program_id(1) == 0)
    def _():
        cu_ref[:, 0:halo, :] = jnp.zeros((cu_ref.shape[0], halo, cu_ref.shape[2]), _F32)

    x = x_ref[...]
    hn = _rms(x, g_ref[layer:layer + 1, :])
    b_gate = _dot(hn, win_ref[:, 0:d])
    cu = _dot(hn, win_ref[:, d:2 * d]) * _dot(hn, win_ref[:, 2 * d:3 * d])
    conv = _causal_conv3(cu_ref, 0, cu, cw_ref[...])
    o_ref[...] = x + _dot(b_gate * conv, wout_ref[...])


def _mixer(x, g, w_in, conv_w, w_out, ffn_w_up, ffn_w_down, layer, a):
    bsz, s, d = x.shape
    tile = ROW_TILE
    grid = (bsz, s // tile)
    cast_in, cast_out, cast_shape = _cast_plan(ffn_w_up, ffn_w_down, layer, grid)
    return pl.pallas_call(
        functools.partial(_mixer_kernel, layer=layer),
        grid=grid,
        in_specs=[_row_spec(tile, d), _const_spec(g.shape), _layer_spec((d, 3 * d), a),
                  _layer_spec((CONV_W, d), a), _layer_spec((d, d), a)] + cast_in,
        out_specs=[_row_spec(tile, d)] + cast_out,
        out_shape=[jax.ShapeDtypeStruct(x.shape, _F32)] + cast_shape,
        scratch_shapes=[pltpu.VMEM((d // V7X_LANES, tile + V7X_SUBLANES, V7X_LANES), _F32)],
        compiler_params=_params(MIXER_VMEM),
        name="sc_mixer",
    )(x, g, w_in, conv_w, w_out, ffn_w_up, ffn_w_down)


def _ffn_kernel(h_ref, g_ref, wup_ref, cw_ref, cb_ref, wdn_ref, fg_ref, o_ref,
                gbuf_ref, act_ref, *, layer, final_norm):
    tile, d = h_ref.shape
    dff = act_ref.shape[1]
    halo = V7X_SUBLANES
    ch = FF_CHUNK

    @pl.when(pl.program_id(1) == 0)
    def _():
        gbuf_ref[:, 0:halo, :] = jnp.zeros((gbuf_ref.shape[0], halo, gbuf_ref.shape[2]), _F32)

    h = h_ref[...]
    hn = _rms(h, g_ref[layer:layer + 1, :]).astype(_BF16)
    for c in range(dff // ch):
        cols = slice(c * ch, (c + 1) * ch)
        g = _dot(hn, wup_ref[:, cols])
        v = _dot(hn, wup_ref[:, dff + c * ch:dff + (c + 1) * ch])
        pre = _causal_conv3(gbuf_ref, c, g, cw_ref[:, cols]) + cb_ref[layer:layer + 1, cols]
        act = pre * (1.0 / (1.0 + jnp.exp(-pre))) * v
        act_ref[:, cols] = act.astype(_BF16)
    out = h + _dot(act_ref[...], wdn_ref[...])
    if final_norm:
        out = _rms(out, fg_ref[...])
    o_ref[...] = out


def _ffn(h, g, w_up, conv_w, conv_b, w_down, final_g, *, layer, final_norm):
    bsz, s, d = h.shape
    dff = w_down.shape[0]
    tile = ROW_TILE
    return pl.pallas_call(
        functools.partial(_ffn_kernel, layer=layer, final_norm=final_norm),
        grid=(bsz, s // tile),
        in_specs=[_row_spec(tile, d), _const_spec(g.shape), _const_spec((d, 2 * dff)),
                  _layer_spec((CONV_W, dff), layer), _const_spec(conv_b.shape),
                  _const_spec((dff, d)), _const_spec((1, d))],
        out_specs=_row_spec(tile, d),
        out_shape=jax.ShapeDtypeStruct(h.shape, _F32),
        scratch_shapes=[pltpu.VMEM((dff // FF_CHUNK, tile + V7X_SUBLANES, FF_CHUNK), _F32),
                        pltpu.VMEM((tile, dff), _BF16)],
        compiler_params=_params(FFN_VMEM),
        name="conv_ffn_final" if final_norm else "conv_ffn",
    )(h, g, w_up, conv_w, conv_b, w_down, final_g)


def _rope_table_t(pos_row, inv_freq_col, scale):
    ang = pos_row.astype(_F32) * inv_freq_col
    cos, sin = jnp.cos(ang), jnp.sin(ang)
    table = jnp.concatenate([cos, cos, -sin, sin], axis=0)
    return table * scale if scale != 1.0 else table


def _kv_kernel(h_ref, pos_ref, freq_ref, gin_ref, wdkv_ref, glat_ref, wkr_ref, wuk_ref,
               wuv_ref, kn_ref, kr_ref, vt_ref):
    hn = _rms(h_ref[...], gin_ref[...])
    c_kv = _rms(_dot(hn, wdkv_ref[...]), glat_ref[...])
    p = _dot(hn, wkr_ref[...]) * _rope_table_t(pos_ref[...], freq_ref[...], 1.0).T
    kr_ref[...] = (p + pltpu.roll(p, QK_ROPE, 1)).astype(_BF16)
    kn_ref[...] = _dot(c_kv, wuk_ref[...]).astype(_BF16)
    vt_ref[...] = _dot(wuv_ref[...].T, c_kv.T).astype(_BF16)


def _shared_kv(h, pos, freq, g_in, w_dkv, g_lat, w_kr2, w_uk, w_uv):
    bsz, s, d = h.shape
    lora = w_dkv.shape[1]
    hk = w_uk.shape[1]
    hv = w_uv.shape[1]
    tile = KV_TILE
    return pl.pallas_call(
        _kv_kernel,
        grid=(bsz, s // tile),
        in_specs=[_row_spec(tile, d), pl.BlockSpec((None, 1, tile), lambda b, i: (b, 0, i)),
                  _const_spec((QK_ROPE // 2, 1)),
                  _const_spec((1, d)), _const_spec((d, lora)), _const_spec((1, lora)),
                  _const_spec((d, 2 * QK_ROPE)), _const_spec((lora, hk)),
                  _const_spec((lora, hv))],
        out_specs=[_row_spec(tile, hk), _row_spec(tile, 2 * QK_ROPE),
                   pl.BlockSpec((None, hv, tile), lambda b, i: (b, 0, i))],
        out_shape=[jax.ShapeDtypeStruct((bsz, s, hk), _BF16),
                   jax.ShapeDtypeStruct((bsz, s, 2 * QK_ROPE), _BF16),
                   jax.ShapeDtypeStruct((bsz, hv, s), _BF16)],
        compiler_params=_params(KV_VMEM),
        name="shared_kv",
    )(h, pos, freq, g_in, w_dkv, g_lat, w_kr2, w_uk, w_uv)


def _mla_kernel(h_ref, posr_ref, freqc_ref, g_ref, wdq_ref, gq_ref, wuqt_ref, wo_ref,
                kn_ref, kr_ref, vt_ref, fup_ref, fdn_ref, o_ref, fup_out, fdn_out,
                qt_ref, ob_ref, s_ref, *, layer, a):
    tq, d = h_ref.shape
    i = pl.program_id(1)
    _cast_block(fup_ref, fdn_ref, fup_out, fdn_out)
    scale = float(QK_NOPE + QK_ROPE) ** -0.5 * math.log2(math.e)

    h = h_ref[...]
    hn = _rms(h, g_ref[layer:layer + 1, :])
    c_q = _rms(_dot(hn, wdq_ref[...]), gq_ref[a:a + 1, :])
    c_qt = c_q.T
    table = _rope_table_t(posr_ref[...], freqc_ref[...], scale)
    for hd in range(N_HEADS):
        qt = _dot(wuqt_ref[hd * HEAD_W:(hd + 1) * HEAD_W, :], c_qt)
        qt_ref[hd * HEAD_W:hd * HEAD_W + QK_NOPE, :] = (qt[0:QK_NOPE] * scale).astype(_BF16)
        qt_ref[hd * HEAD_W + QK_NOPE:(hd + 1) * HEAD_W, :] = (qt[QK_NOPE:] * table).astype(_BF16)

    ones_rows = 2 * V7X_SUBLANES
    kb = KEY_BLOCK
    n_sub = tq // kb
    k_chunk = lax.broadcasted_iota(jnp.int32, (kb, kb), 0) // CHUNK
    q_chunk = lax.broadcasted_iota(jnp.int32, (kb, kb), 1) // CHUNK
    diag_mask = k_chunk <= q_chunk
    ones = jnp.ones((ones_rows, kb), _BF16)

    def attend(c):
        n_blocks = (c + 1) * n_sub

        def first_lane(t):
            return max(t - c * n_sub, 0) * kb

        def score_block(hd, t, m8):
            rows = slice(t * kb, (t + 1) * kb)
            lo = first_lane(t)
            k = jnp.concatenate([kn_ref[rows, hd * QK_NOPE:(hd + 1) * QK_NOPE], kr_ref[rows, :]],
                                axis=1)
            s = _dot(k, qt_ref[hd * HEAD_W:(hd + 1) * HEAD_W, lo:tq])
            if t >= c * n_sub:
                s_diag = jnp.where(diag_mask, s[:, 0:kb], NEG_INF)
                s = jnp.concatenate([s_diag, s[:, kb:]], axis=1) if lo + kb < tq else s_diag
            s_ref[hd % 2, rows, lo:tq] = s
            blk_max = jnp.max(s.reshape(kb // V7X_SUBLANES, V7X_SUBLANES, tq - lo), axis=0)
            if m8 is None:
                return blk_max
            if lo == 0:
                return jnp.maximum(m8, blk_max)
            return jnp.concatenate([m8[:, 0:lo], jnp.maximum(m8[:, lo:], blk_max)], axis=1)

        def pv_block(hd, t, m, acc):
            rows = slice(t * kb, (t + 1) * kb)
            lo = first_lane(t)
            p = jnp.exp2(s_ref[hd % 2, rows, lo:tq] - m[:, lo:]).astype(_BF16)
            v_aug = jnp.concatenate([vt_ref[hd * V_HEAD:(hd + 1) * V_HEAD, rows], ones], axis=0)
            o_aug = _dot(v_aug, p)
            if acc is None:
                return o_aug
            if lo == 0:
                return acc + o_aug
            return jnp.concatenate([acc[:, 0:lo], acc[:, lo:] + o_aug], axis=1)

        m8_next = None
        for t in range(n_blocks):
            m8_next = score_block(0, t, m8_next)
        for hd in range(N_HEADS):
            m = jnp.max(m8_next, axis=0, keepdims=True)
            m8_next, acc = None, None
            for t in range(n_blocks):
                if hd + 1 < N_HEADS:
                    m8_next = score_block(hd + 1, t, m8_next)
                acc = pv_block(hd, t, m, acc)
            o_t = acc[0:V_HEAD] * (1.0 / acc[V_HEAD:V_HEAD + 1])
            ob_ref[:, hd * V_HEAD:(hd + 1) * V_HEAD] = o_t.T

    for c in range(kn_ref.shape[0] // tq):
        pl.when(i == c)(functools.partial(attend, c))

    o_ref[...] = h + _dot(ob_ref[...], wo_ref[...])


def _mla(h, pos_row, freq_col, g, w_dq, g_q, w_uq2_t, w_o, k_nope, k_rope, v_t,
         ffn_w_up, ffn_w_down, layer, a):
    bsz, s, d = h.shape
    qlora = w_dq.shape[2]
    hk = k_nope.shape[2]
    hv = v_t.shape[1]
    tq = Q_TILE
    grid = (bsz, s // tq)
    cast_in, cast_out, cast_shape = _cast_plan(ffn_w_up, ffn_w_down, layer, grid)
    return pl.pallas_call(
        functools.partial(_mla_kernel, layer=layer, a=a),
        grid=grid,
        in_specs=[_row_spec(tq, d), pl.BlockSpec((None, 1, tq), lambda b, i: (b, 0, i)),
                  _const_spec((QK_ROPE // 2, 1)),
                  _const_spec(g.shape), _layer_spec((d, qlora), a), _const_spec(g_q.shape),
                  _const_spec((N_HEADS * HEAD_W, qlora)), _layer_spec((hv, d), a),
                  pl.BlockSpec((None, s, hk), lambda b, i: (b, 0, 0)),
                  pl.BlockSpec((None, s, 2 * QK_ROPE), lambda b, i: (b, 0, 0)),
                  pl.BlockSpec((None, hv, s), lambda b, i: (b, 0, 0))] + cast_in,
        out_specs=[_row_spec(tq, d)] + cast_out,
        out_shape=[jax.ShapeDtypeStruct(h.shape, _F32)] + cast_shape,
        scratch_shapes=[pltpu.VMEM((N_HEADS * HEAD_W, tq), _BF16),
                        pltpu.VMEM((tq, hv), _F32),
                        pltpu.VMEM((2, s, tq), _F32)],
        compiler_params=_params(MLA_VMEM),
        name="mla_attention",
    )(h, pos_row, freq_col, g, w_dq, g_q, w_uq2_t, w_o, k_nope, k_rope, v_t,
      ffn_w_up, ffn_w_down)


def _swap_halves(w):
    half = w.shape[-1] // 2
    return jnp.concatenate([w[..., half:], w[..., :half]], axis=-1)


def kernel(x, positions, attn_norm, ffn_norm, final_norm, sc_w_in, sc_conv_w, sc_w_out, kv_in_norm, w_dkv, kv_latent_norm, w_kr, w_uk, w_uv, w_dq, q_latent_norm, w_uq, w_o, ffn_w_up, ffn_conv_w, ffn_conv_b, ffn_w_down):
    depth = attn_norm.shape[0]
    n_a = sc_w_in.shape[0]
    row = lambda v: v.reshape(1, -1)

    half = QK_ROPE // 2
    inv_freq = 1.0 / (ROPE_THETA ** (jnp.arange(half, dtype=_F32) / half))
    freq_col = inv_freq.reshape(half, 1)
    pos_row = positions.reshape(positions.shape[0], 1, positions.shape[1])

    w_kr2 = jnp.concatenate([w_kr, _swap_halves(w_kr)], axis=1)
    final_g = row(final_norm)

    h = x
    kv = None
    for layer in range(depth):
        if layer < n_a:
            h, w_up, w_down = _mixer(h, attn_norm, sc_w_in, sc_conv_w, sc_w_out,
                                     ffn_w_up, ffn_w_down, layer, layer)
        else:
            a = layer - n_a
            wq = w_uq[a].reshape(-1, N_HEADS, QK_NOPE + QK_ROPE)
            rope_cols = wq[:, :, QK_NOPE:]
            w_uq2 = jnp.concatenate([wq, _swap_halves(rope_cols)], axis=2)
            w_uq2_t = w_uq2.reshape(-1, N_HEADS * HEAD_W).T
            h, w_up, w_down = _mla(h, pos_row, freq_col, attn_norm, w_dq, q_latent_norm,
                                   w_uq2_t, w_o, *kv, ffn_w_up, ffn_w_down, layer, a)
        h = _ffn(h, ffn_norm, w_up, ffn_conv_w, ffn_conv_b, w_down, final_g,
                 layer=layer, final_norm=(layer == depth - 1))
        if layer == n_a - 1:
            kv = _shared_kv(h, pos_row, freq_col, row(kv_in_norm), w_dkv, row(kv_latent_norm),
                            w_kr2, w_uk, w_uv)
    return h
```

```python
import functools
import math

import jax
import jax.numpy as jnp
from jax import lax
from jax.experimental import pallas as pl
from jax.experimental.pallas import tpu as pltpu

CHUNK = 64
CONV_W = 3
N_HEADS = 8
QK_NOPE = 128
QK_ROPE = 64
V_HEAD = 128
ROPE_THETA = 10000.0
EPS = 1e-6
NEG_INF = -1e30

V7X_SUBLANES = 8
V7X_LANES = 128
V7X_MXU_DIM = 256
V7X_VMEM_BYTES = 64 * 1024 * 1024

HEAD_W = QK_NOPE + 2 * QK_ROPE
ROW_TILE = 512
MIXER_TILE = 1024
KV_TILE = 2048
Q_TILE = 512
KEY_BLOCK = V7X_MXU_DIM
FF_CHUNK = V7X_MXU_DIM
VMEM_LIMIT = V7X_VMEM_BYTES - 8 * 1024 * 1024
FFN_VMEM_LIMIT = 38 * 1024 * 1024

_BF16 = jnp.bfloat16
_F32 = jnp.float32


def _dot(a, b):
    return jnp.dot(a, b, preferred_element_type=_F32)


def _rms(x, g):
    return x * lax.rsqrt(jnp.mean(x * x, axis=-1, keepdims=True) + EPS) * g


def _const_spec(shape):
    nd = len(shape)
    return pl.BlockSpec(shape, lambda *_: (0,) * nd, pipeline_mode=pl.Buffered(1))


def _layer_spec(shape, layer):
    nd = len(shape)
    return pl.BlockSpec((None,) + tuple(shape), lambda *_: (layer,) + (0,) * nd,
                        pipeline_mode=pl.Buffered(1))


def _row_spec(tile, width):
    return pl.BlockSpec((None, tile, width), lambda b, i: (b, i, 0))


def _cast_plan(w_up, w_down, layer, grid):
    bf16_rows = 2 * V7X_SUBLANES
    n_steps = grid[0] * grid[1]
    _, d, up_cols = w_up.shape
    _, dff, dn_cols = w_down.shape
    up_rows = d // n_steps
    every = next(k for k in range(1, n_steps + 1)
                 if n_steps % k == 0 and (dff * k) % (n_steps * bf16_rows) == 0)
    dn_rows = dff * every // n_steps
    assert up_rows * n_steps == d and up_rows % bf16_rows == 0
    step = lambda b, i: b * grid[1] + i
    in_specs = [pl.BlockSpec((None, up_rows, up_cols), lambda b, i: (layer, step(b, i), 0)),
                pl.BlockSpec((None, dn_rows, dn_cols), lambda b, i: (layer, step(b, i) // every, 0))]
    out_specs = [pl.BlockSpec((up_rows, up_cols), lambda b, i: (step(b, i), 0)),
                 pl.BlockSpec((dn_rows, dn_cols), lambda b, i: (step(b, i) // every, 0))]
    out_shape = [jax.ShapeDtypeStruct((d, up_cols), _BF16),
                 jax.ShapeDtypeStruct((dff, dn_cols), _BF16)]
    return in_specs, out_specs, out_shape


def _cast_block(src_up, src_dn, dst_up, dst_dn):
    dst_up[...] = src_up[...].astype(_BF16)
    dst_dn[...] = src_dn[...].astype(_BF16)


def _params(vmem_bytes=VMEM_LIMIT):
    return pltpu.CompilerParams(
        dimension_semantics=("arbitrary", "arbitrary"), vmem_limit_bytes=vmem_bytes)


def _causal_conv3(slab_ref, first, cur, w):
    halo, lanes = V7X_SUBLANES, slab_ref.shape[2]
    tile = cur.shape[0]
    outs = []
    for j in range(cur.shape[1] // lanes):
        cols = slice(j * lanes, (j + 1) * lanes)
        slab = slab_ref.at[first + j]
        slab[halo:halo + tile, :] = cur[:, cols]
        y = slab[pl.ds(halo - 2, tile), :] * w[0:1, cols]
        y = y + slab[pl.ds(halo - 1, tile), :] * w[1:2, cols]
        outs.append(y + cur[:, cols] * w[2:3, cols])
        slab[0:halo, :] = slab[tile:tile + halo, :]
    return jnp.concatenate(outs, axis=1)


def _mixer_kernel(x_ref, g_ref, win_ref, cw_ref, wout_ref, fup_ref, fdn_ref,
                  o_ref, fup_out, fdn_out, cu_ref, *, layer):
    tile, d = x_ref.shape
    halo = V7X_SUBLANES
    _cast_block(fup_ref, fdn_ref, fup_out, fdn_out)

    @pl.when(pl.program_id(1) == 0)
    def _():
        cu_ref[:, 0:halo, :] = jnp.zeros((cu_ref.shape[0], halo, cu_ref.shape[2]), _F32)

    x = x_ref[...]
    hn = _rms(x, g_ref[layer:layer + 1, :])
    b_gate = _dot(hn, win_ref[:, 0:d])
    cu = _dot(hn, win_ref[:, d:2 * d]) * _dot(hn, win_ref[:, 2 * d:3 * d])
    conv = _causal_conv3(cu_ref, 0, cu, cw_ref[...])
    o_ref[...] = x + _dot(b_gate * conv, wout_ref[...])


def _mixer(x, g, w_in, conv_w, w_out, ffn_w_up, ffn_w_down, layer, a):
    bsz, s, d = x.shape
    tile = MIXER_TILE
    grid = (bsz, s // tile)
    cast_in, cast_out, cast_shape = _cast_plan(ffn_w_up, ffn_w_down, layer, grid)
    return pl.pallas_call(
        functools.partial(_mixer_kernel, layer=layer),
        grid=grid,
        in_specs=[_row_spec(tile, d), _const_spec(g.shape), _layer_spec((d, 3 * d), a),
                  _layer_spec((CONV_W, d), a), _layer_spec((d, d), a)] + cast_in,
        out_specs=[_row_spec(tile, d)] + cast_out,
        out_shape=[jax.ShapeDtypeStruct(x.shape, _F32)] + cast_shape,
        scratch_shapes=[pltpu.VMEM((d // V7X_LANES, tile + V7X_SUBLANES, V7X_LANES), _F32)],
        compiler_params=_params(),
        name="sc_mixer",
    )(x, g, w_in, conv_w, w_out, ffn_w_up, ffn_w_down)


def _ffn_kernel(h_ref, g_ref, wup_ref, cw_ref, cb_ref, wdn_ref, fg_ref, o_ref,
                gbuf_ref, act_ref, *, layer, final_norm):
    tile, d = h_ref.shape
    dff = act_ref.shape[1]
    halo = V7X_SUBLANES
    ch = FF_CHUNK

    @pl.when(pl.program_id(1) == 0)
    def _():
        gbuf_ref[:, 0:halo, :] = jnp.zeros((gbuf_ref.shape[0], halo, gbuf_ref.shape[2]), _F32)

    h = h_ref[...]
    hn = _rms(h, g_ref[layer:layer + 1, :]).astype(_BF16)
    for c in range(dff // ch):
        cols = slice(c * ch, (c + 1) * ch)
        g = _dot(hn, wup_ref[:, cols])
        v = _dot(hn, wup_ref[:, dff + c * ch:dff + (c + 1) * ch])
        pre = _causal_conv3(gbuf_ref, c, g, cw_ref[:, cols]) + cb_ref[layer:layer + 1, cols]
        act = pre * (1.0 / (1.0 + jnp.exp(-pre))) * v
        act_ref[:, cols] = act.astype(_BF16)
    out = h + _dot(act_ref[...], wdn_ref[...])
    if final_norm:
        out = _rms(out, fg_ref[...])
    o_ref[...] = out


def _ffn(h, g, w_up, conv_w, conv_b, w_down, final_g, *, layer, final_norm):
    bsz, s, d = h.shape
    dff = w_down.shape[0]
    tile = ROW_TILE
    return pl.pallas_call(
        functools.partial(_ffn_kernel, layer=layer, final_norm=final_norm),
        grid=(bsz, s // tile),
        in_specs=[_row_spec(tile, d), _const_spec(g.shape), _const_spec((d, 2 * dff)),
                  _layer_spec((CONV_W, dff), layer), _const_spec(conv_b.shape),
                  _const_spec((dff, d)), _const_spec((1, d))],
        out_specs=_row_spec(tile, d),
        out_shape=jax.ShapeDtypeStruct(h.shape, _F32),
        scratch_shapes=[pltpu.VMEM((dff // FF_CHUNK, tile + V7X_SUBLANES, FF_CHUNK), _F32),
                        pltpu.VMEM((tile, dff), _BF16)],
        compiler_params=_params(FFN_VMEM_LIMIT),
        name="conv_ffn_final" if final_norm else "conv_ffn",
    )(h, g, w_up, conv_w, conv_b, w_down, final_g)


def _rope_table_t(pos_row, inv_freq_col, scale):
    ang = pos_row.astype(_F32) * inv_freq_col
    cos, sin = jnp.cos(ang), jnp.sin(ang)
    table = jnp.concatenate([cos, cos, -sin, sin], axis=0)
    return table * scale if scale != 1.0 else table


def _kv_kernel(h_ref, pos_ref, freq_ref, gin_ref, wdkv_ref, glat_ref, wkr_ref, wuk_ref,
               wuv_ref, kn_ref, kr_ref, vt_ref):
    hn = _rms(h_ref[...], gin_ref[...])
    c_kv = _rms(_dot(hn, wdkv_ref[...]), glat_ref[...])
    p = _dot(hn, wkr_ref[...]) * _rope_table_t(pos_ref[...], freq_ref[...], 1.0).T
    kr_ref[...] = (p + pltpu.roll(p, QK_ROPE, 1)).astype(_BF16)
    kn_ref[...] = _dot(c_kv, wuk_ref[...]).astype(_BF16)
    vt_ref[...] = _dot(wuv_ref[...].T, c_kv.T).astype(_BF16)


def _shared_kv(h, pos, freq, g_in, w_dkv, g_lat, w_kr2, w_uk, w_uv):
    bsz, s, d = h.shape
    lora = w_dkv.shape[1]
    hk = w_uk.shape[1]
    hv = w_uv.shape[1]
    tile = KV_TILE
    return pl.pallas_call(
        _kv_kernel,
        grid=(bsz, s // tile),
        in_specs=[_row_spec(tile, d), pl.BlockSpec((None, 1, tile), lambda b, i: (b, 0, i)),
                  _const_spec((QK_ROPE // 2, 1)),
                  _const_spec((1, d)), _const_spec((d, lora)), _const_spec((1, lora)),
                  _const_spec((d, 2 * QK_ROPE)), _const_spec((lora, hk)),
                  _const_spec((lora, hv))],
        out_specs=[_row_spec(tile, hk), _row_spec(tile, 2 * QK_ROPE),
                   pl.BlockSpec((None, hv, tile), lambda b, i: (b, 0, i))],
        out_shape=[jax.ShapeDtypeStruct((bsz, s, hk), _BF16),
                   jax.ShapeDtypeStruct((bsz, s, 2 * QK_ROPE), _BF16),
                   jax.ShapeDtypeStruct((bsz, hv, s), _BF16)],
        compiler_params=_params(),
        name="shared_kv",
    )(h, pos, freq, g_in, w_dkv, g_lat, w_kr2, w_uk, w_uv)


def _mla_kernel(h_ref, posr_ref, freqc_ref, g_ref, wdq_ref, gq_ref, wuqt_ref, wo_ref,
                kn_ref, kr_ref, vt_ref, fup_ref, fdn_ref, o_ref, fup_out, fdn_out,
                qt_ref, ob_ref, s_ref, *, layer, a):
    tq, d = h_ref.shape
    i = pl.program_id(1)
    _cast_block(fup_ref, fdn_ref, fup_out, fdn_out)
    scale = float(QK_NOPE + QK_ROPE) ** -0.5 * math.log2(math.e)

    h = h_ref[...]
    hn = _rms(h, g_ref[layer:layer + 1, :])
    c_q = _rms(_dot(hn, wdq_ref[...]), gq_ref[a:a + 1, :])
    c_qt = c_q.T
    table = _rope_table_t(posr_ref[...], freqc_ref[...], scale)
    for hd in range(N_HEADS):
        qt = _dot(wuqt_ref[hd * HEAD_W:(hd + 1) * HEAD_W, :], c_qt)
        qt_ref[hd * HEAD_W:hd * HEAD_W + QK_NOPE, :] = (qt[0:QK_NOPE] * scale).astype(_BF16)
        qt_ref[hd * HEAD_W + QK_NOPE:(hd + 1) * HEAD_W, :] = (qt[QK_NOPE:] * table).astype(_BF16)

    ones_rows = 2 * V7X_SUBLANES
    kb = KEY_BLOCK
    n_sub = tq // kb
    k_chunk = lax.broadcasted_iota(jnp.int32, (kb, kb), 0) // CHUNK
    q_chunk = lax.broadcasted_iota(jnp.int32, (kb, kb), 1) // CHUNK
    diag_mask = k_chunk <= q_chunk
    ones = jnp.ones((ones_rows, kb), _BF16)

    def attend(c):
        n_blocks = (c + 1) * n_sub

        def first_lane(t):
            return max(t - c * n_sub, 0) * kb

        def score_block(hd, t, m8):
            rows = slice(t * kb, (t + 1) * kb)
            lo = first_lane(t)
            k = jnp.concatenate([kn_ref[rows, hd * QK_NOPE:(hd + 1) * QK_NOPE], kr_ref[rows, :]],
                                axis=1)
            s = _dot(k, qt_ref[hd * HEAD_W:(hd + 1) * HEAD_W, lo:tq])
            if t >= c * n_sub:
                s_diag = jnp.where(diag_mask, s[:, 0:kb], NEG_INF)
                s = jnp.concatenate([s_diag, s[:, kb:]], axis=1) if lo + kb < tq else s_diag
            s_ref[hd % 2, rows, lo:tq] = s
            blk_max = jnp.max(s.reshape(kb // V7X_SUBLANES, V7X_SUBLANES, tq - lo), axis=0)
            if m8 is None:
                return blk_max
            if lo == 0:
                return jnp.maximum(m8, blk_max)
            return jnp.concatenate([m8[:, 0:lo], jnp.maximum(m8[:, lo:], blk_max)], axis=1)

        def pv_block(hd, t, m, acc):
            rows = slice(t * kb, (t + 1) * kb)
            lo = first_lane(t)
            p = jnp.exp2(s_ref[hd % 2, rows, lo:tq] - m[:, lo:]).astype(_BF16)
            v_aug = jnp.concatenate([vt_ref[hd * V_HEAD:(hd + 1) * V_HEAD, rows], ones], axis=0)
            o_aug = _dot(v_aug, p)
            if acc is None:
                return o_aug
            if lo == 0:
                return acc + o_aug
            return jnp.concatenate([acc[:, 0:lo], acc[:, lo:] + o_aug], axis=1)

        m8_next = None
        for t in range(n_blocks):
            m8_next = score_block(0, t, m8_next)
        for hd in range(N_HEADS):
            m = jnp.max(m8_next, axis=0, keepdims=True)
            m8_next, acc = None, None
            for t in range(n_blocks):
                if hd + 1 < N_HEADS:
                    m8_next = score_block(hd + 1, t, m8_next)
                acc = pv_block(hd, t, m, acc)
            o_t = acc[0:V_HEAD] * (1.0 / acc[V_HEAD:V_HEAD + 1])
            ob_ref[:, hd * V_HEAD:(hd + 1) * V_HEAD] = o_t.T

    for c in range(kn_ref.shape[0] // tq):
        pl.when(i == c)(functools.partial(attend, c))

    o_ref[...] = h + _dot(ob_ref[...], wo_ref[...])


def _mla(h, pos_row, freq_col, g, w_dq, g_q, w_uq2_t, w_o, k_nope, k_rope, v_t,
         ffn_w_up, ffn_w_down, layer, a):
    bsz, s, d = h.shape
    qlora = w_dq.shape[2]
    hk = k_nope.shape[2]
    hv = v_t.shape[1]
    tq = Q_TILE
    grid = (bsz, s // tq)
    cast_in, cast_out, cast_shape = _cast_plan(ffn_w_up, ffn_w_down, layer, grid)
    return pl.pallas_call(
        functools.partial(_mla_kernel, layer=layer, a=a),
        grid=grid,
        in_specs=[_row_spec(tq, d), pl.BlockSpec((None, 1, tq), lambda b, i: (b, 0, i)),
                  _const_spec((QK_ROPE // 2, 1)),
                  _const_spec(g.shape), _layer_spec((d, qlora), a), _const_spec(g_q.shape),
                  _const_spec((N_HEADS * HEAD_W, qlora)), _layer_spec((hv, d), a),
                  pl.BlockSpec((None, s, hk), lambda b, i: (b, 0, 0)),
                  pl.BlockSpec((None, s, 2 * QK_ROPE), lambda b, i: (b, 0, 0)),
                  pl.BlockSpec((None, hv, s), lambda b, i: (b, 0, 0))] + cast_in,
        out_specs=[_row_spec(tq, d)] + cast_out,
        out_shape=[jax.ShapeDtypeStruct(h.shape, _F32)] + cast_shape,
        scratch_shapes=[pltpu.VMEM((N_HEADS * HEAD_W, tq), _BF16),
                        pltpu.VMEM((tq, hv), _F32),
                        pltpu.VMEM((2, s, tq), _F32)],
        compiler_params=_params(),
        name="mla_attention",
    )(h, pos_row, freq_col, g, w_dq, g_q, w_uq2_t, w_o, k_nope, k_rope, v_t,
      ffn_w_up, ffn_w_down)


def _swap_halves(w):
    half = w.shape[-1] // 2
    return jnp.concatenate([w[..., half:], w[..., :half]], axis=-1)


def kernel(x, positions, attn_norm, ffn_norm, final_norm, sc_w_in, sc_conv_w, sc_w_out, kv_in_norm, w_dkv, kv_latent_norm, w_kr, w_uk, w_uv, w_dq, q_latent_norm, w_uq, w_o, ffn_w_up, ffn_conv_w, ffn_conv_b, ffn_w_down):
    depth = attn_norm.shape[0]
    n_a = sc_w_in.shape[0]
    row = lambda v: v.reshape(1, -1)

    half = QK_ROPE // 2
    inv_freq = 1.0 / (ROPE_THETA ** (jnp.arange(half, dtype=_F32) / half))
    freq_col = inv_freq.reshape(half, 1)
    pos_row = positions.reshape(positions.shape[0], 1, positions.shape[1])

    w_kr2 = jnp.concatenate([w_kr, _swap_halves(w_kr)], axis=1)
    final_g = row(final_norm)

    h = x
    kv = None
    for layer in range(depth):
        if layer < n_a:
            h, w_up, w_down = _mixer(h, attn_norm, sc_w_in, sc_conv_w, sc_w_out,
                                     ffn_w_up, ffn_w_down, layer, layer)
        else:
            a = layer - n_a
            wq = w_uq[a].reshape(-1, N_HEADS, QK_NOPE + QK_ROPE)
            rope_cols = wq[:, :, QK_NOPE:]
            w_uq2 = jnp.concatenate([wq, _swap_halves(rope_cols)], axis=2)
            w_uq2_t = w_uq2.reshape(-1, N_HEADS * HEAD_W).T
            h, w_up, w_down = _mla(h, pos_row, freq_col, attn_norm, w_dq, q_latent_norm,
                                   w_uq2_t, w_o, *kv, ffn_w_up, ffn_w_down, layer, a)
        h = _ffn(h, ffn_norm, w_up, ffn_conv_w, ffn_conv_b, w_down, final_g,
                 layer=layer, final_norm=(layer == depth - 1))
        if layer == n_a - 1:
            kv = _shared_kv(h, pos_row, freq_col, row(kv_in_norm), w_dkv, row(kv_latent_norm),
                            w_kr2, w_uk, w_uv)
    return h
```
